```python
import jax, jax.numpy as jnp
from jax import lax
import numpy as np

D_MODEL = 1024
BATCH = 32
SEQ = 2048
DEPTH = 1
DEC_BATCH = 16
DEC_SEQ = 16
PAST_LEN = 4096

CHUNK = 64
Q_BLOCK = 128
RET_HEADS = 4
RET_KEY_DIM = 128
RET_VAL_DIM = 256
FOX_HEADS = 8
FOX_HEAD_DIM = 64
D_FF = 2816
PLE_DIM = 256
N_BRANCH = 2
ROPE_BASE = 10000.0
EPS = 1e-6
RET_QK_W = RET_HEADS * RET_KEY_DIM
RET_V_W = RET_HEADS * RET_VAL_DIM
FOX_W = FOX_HEADS * FOX_HEAD_DIM
IN_SIZES = (RET_QK_W, RET_QK_W, RET_V_W, RET_V_W, FOX_W, FOX_W, FOX_W, FOX_HEADS, N_BRANCH * D_MODEL)
IN_COLS = RET_QK_W * 2 + RET_V_W * 2 + FOX_W * 3 + FOX_HEADS + N_BRANCH * D_MODEL

kernel_name = "hybrid_retention_fox_streaming_step"


def rms_norm(x, g=None):
    xf = x.astype(jnp.float32)
    y = xf * lax.rsqrt(jnp.mean(xf * xf, axis=-1, keepdims=True) + EPS)
    if g is not None:
        y = y * g.astype(jnp.float32)
    return y.astype(x.dtype)


def swiglu(x, w_in, w_out):
    gu = x @ w_in
    g, u = gu[..., :D_FF], gu[..., D_FF:]
    return (jax.nn.silu(g) * u) @ w_out


def rotary(x, pos):
    d = x.shape[-1]
    half = d // 2
    freqs = ROPE_BASE ** (-jnp.arange(half, dtype=jnp.float32) / half)
    ang = pos[:, None] * freqs[None, :]
    c = jnp.cos(ang)[None, :, None, :]
    s = jnp.sin(ang)[None, :, None, :]
    xf = x.astype(jnp.float32)
    x1, x2 = xf[..., :half], xf[..., half:]
    return jnp.concatenate([x1 * c - x2 * s, x1 * s + x2 * c], axis=-1).astype(x.dtype)


def retention(q, k, v, s0, chunk_len):
    B, S, H, dk = q.shape
    dv = v.shape[-1]
    nc = S // chunk_len
    log_gamma = jnp.log(1.0 - 2.0 ** (-5.0 - jnp.arange(H, dtype=jnp.float32)))
    idx = jnp.arange(chunk_len, dtype=jnp.float32)
    intra_decay = jnp.exp(log_gamma[:, None, None] * jnp.abs(idx[:, None] - idx[None, :]))
    inter_decay = jnp.exp(log_gamma[None, :] * (idx[:, None] + 1.0))[None, :, :, None]
    kv_decay = jnp.exp(log_gamma[None, :] * (chunk_len - 1.0 - idx[:, None]))[None, :, :, None]
    chunk_decay = jnp.exp(log_gamma * chunk_len)[None, :, None, None]

    def to_chunks(a):
        return a.reshape(B, nc, chunk_len, H, a.shape[-1]).transpose(1, 0, 2, 3, 4)

    def step(state, qkv):
        qc, kc, vc = (a.astype(jnp.float32) for a in qkv)
        inter = jnp.einsum('bchk,bhkv->bchv', qc, state) * inter_decay
        att = jnp.einsum('bihk,bjhk->bhij', qc, kc) * intra_decay[None]
        intra = jnp.einsum('bhij,bjhv->bihv', att, vc)
        new_state = state * chunk_decay + jnp.einsum('bjhk,bjhv->bhkv', kc * kv_decay, vc)
        return new_state, (inter + intra).astype(v.dtype)

    s_final, o = lax.scan(step, s0.astype(jnp.float32), (to_chunks(q), to_chunks(k), to_chunks(v)))
    o = o.transpose(1, 0, 2, 3, 4).reshape(B, S, H, dv)
    return o, s_final.astype(s0.dtype)


def fox_prompt(q, k, v, logf):
    B, S, H, d = q.shape
    scale = d ** -0.5
    c = jnp.cumsum(logf, axis=1).transpose(0, 2, 1)
    outs = []
    for i0 in range(0, S, Q_BLOCK):
        i1 = i0 + Q_BLOCK
        s = jnp.einsum('bqhd,bkhd->bhqk', q[:, i0:i1], k[:, :i1], preferred_element_type=jnp.float32) * scale
        bias = c[:, :, i0:i1, None] - c[:, :, None, :i1]
        mask = (i0 + jnp.arange(Q_BLOCK))[:, None] >= jnp.arange(i1)[None, :]
        w = jax.nn.softmax(jnp.where(mask, s + bias, -jnp.inf), axis=-1)
        outs.append(jnp.einsum('bhqk,bkhd->bqhd', w.astype(v.dtype), v[:, :i1]))
    return jnp.concatenate(outs, axis=1)


def fox_sample(q, k, v, logf, ck, cv, clogf):
    L = q.shape[1]
    P = ck.shape[1]
    scale = q.shape[-1] ** -0.5
    cn = jnp.cumsum(logf, axis=1).transpose(0, 2, 1)
    clf = clogf.astype(jnp.float32)
    suf = (lax.cumsum(clf, axis=1, reverse=True) - clf).transpose(0, 2, 1)
    s_past = jnp.einsum('bqhd,bkhd->bhqk', q, ck.astype(q.dtype), preferred_element_type=jnp.float32) * scale
    s_past = s_past + cn[..., :, None] + suf[..., None, :]
    s_new = jnp.einsum('bqhd,bkhd->bhqk', q, k, preferred_element_type=jnp.float32) * scale
    s_new = s_new + cn[..., :, None] - cn[..., None, :]
    mask = jnp.arange(L)[:, None] >= jnp.arange(L)[None, :]
    s_new = jnp.where(mask, s_new, -jnp.inf)
    w = jax.nn.softmax(jnp.concatenate([s_past, s_new], axis=-1), axis=-1).astype(v.dtype)
    return (jnp.einsum('bhqk,bkhd->bqhd', w[..., :P], cv.astype(v.dtype))
            + jnp.einsum('bhqk,bkhd->bqhd', w[..., P:], v))


def token_mixer(x, pos, ret_s0, fox_cache, prm, ret_chunk):
    B, S, _ = x.shape
    h = rms_norm(x, prm['norm_mix_g'])
    z = h @ prm['w_in_mix']
    parts = []
    off = 0
    for n in IN_SIZES:
        parts.append(z[..., off:off + n])
        off += n
    rq, rk, rv, rg, fq, fk, fv, ff, zg = parts
    q = rotary(rq.reshape(B, S, RET_HEADS, RET_KEY_DIM), pos)
    k = rotary(rk.reshape(B, S, RET_HEADS, RET_KEY_DIM), pos) * (RET_KEY_DIM ** -0.5)
    v = rv.reshape(B, S, RET_HEADS, RET_VAL_DIM)
    o_ret, ret_state = retention(q, k, v, ret_s0, ret_chunk)
    o_ret = rms_norm(o_ret).reshape(B, S, RET_V_W) * jax.nn.silu(rg)
    br_ret = o_ret @ prm['w_br_ret']
    fq = rms_norm(fq.reshape(B, S, FOX_HEADS, FOX_HEAD_DIM), prm['q_norm_g'])
    fk = rms_norm(fk.reshape(B, S, FOX_HEADS, FOX_HEAD_DIM), prm['k_norm_g'])
    fv = fv.reshape(B, S, FOX_HEADS, FOX_HEAD_DIM)
    logf = jax.nn.log_sigmoid(ff.astype(jnp.float32) + prm['b_forget'].astype(jnp.float32))
    if fox_cache is None:
        o_fox = fox_prompt(fq, fk, fv, logf)
    else:
        o_fox = fox_sample(fq, fk, fv, logf, *fox_cache)
    br_fox = o_fox.reshape(B, S, FOX_W) @ prm['w_br_fox']
    g = jax.nn.sigmoid(zg).reshape(B, S, N_BRANCH, D_MODEL)
    y = (g[..., 0, :] * br_ret + g[..., 1, :] * br_fox) @ prm['w_out']
    return y, ret_state, fk, fv, logf


def layer(x, p, pos, ret_s0, fox_cache, prm, ret_chunk):
    x = x + 0.5 * swiglu(rms_norm(x, prm['norm_ffn1_g']), prm['ffn1_w_in'], prm['ffn1_w_out'])
    y, ret_state, fk, fv, logf = token_mixer(x, pos, ret_s0, fox_cache, prm, ret_chunk)
    x = x + y
    x = x + 0.5 * swiglu(rms_norm(x, prm['norm_ffn2_g']), prm['ffn2_w_in'], prm['ffn2_w_out'])
    gate = jax.nn.sigmoid(rms_norm(x, prm['norm_ple_g']) @ prm['w_ple_gate'])
    x = x + (p @ prm['w_ple']) * gate
    return x, ret_state, fk, fv, logf


def setup_inputs(seed: int = 0) -> dict:
    key = jax.random.key(seed)
    ks = jax.random.split(key, 32)
    f32 = jnp.float32

    def nrm(k, shape, scale):
        return jax.random.normal(k, shape, f32) * scale

    def gain(k, shape):
        return 1.0 + 0.05 * jax.random.normal(k, shape, f32)

    return {
        'x_prompt': nrm(ks[0], (BATCH, SEQ, D_MODEL), 1.0),
        'x_sample': nrm(ks[1], (DEC_BATCH, DEC_SEQ, D_MODEL), 1.0),
        'p_prompt': nrm(ks[2], (DEPTH, BATCH, SEQ, PLE_DIM), 1.0),
        'p_sample': nrm(ks[3], (DEPTH, DEC_BATCH, DEC_SEQ, PLE_DIM), 1.0),
        'state_ret': nrm(ks[4], (DEPTH, DEC_BATCH, RET_HEADS, RET_KEY_DIM, RET_VAL_DIM), 0.1),
        'cache_fox_k': nrm(ks[5], (DEPTH, DEC_BATCH, PAST_LEN, FOX_HEADS, FOX_HEAD_DIM), 1.0),
        'cache_fox_v': nrm(ks[6], (DEPTH, DEC_BATCH, PAST_LEN, FOX_HEADS, FOX_HEAD_DIM), 1.0),
        'cache_fox_logf': jax.nn.log_sigmoid(2.0 + jax.random.normal(ks[7], (DEPTH, DEC_BATCH, PAST_LEN, FOX_HEADS), f32)),
        'norm_ffn1_g': gain(ks[8], (DEPTH, D_MODEL)),
        'ffn1_w_in': nrm(ks[9], (DEPTH, D_MODEL, 2 * D_FF), D_MODEL ** -0.5),
        'ffn1_w_out': nrm(ks[10], (DEPTH, D_FF, D_MODEL), D_FF ** -0.5),
        'norm_mix_g': gain(ks[11], (DEPTH, D_MODEL)),
        'w_in_mix': nrm(ks[12], (DEPTH, D_MODEL, IN_COLS), D_MODEL ** -0.5),
        'b_forget': 1.0 + 0.1 * jax.random.normal(ks[13], (DEPTH, FOX_HEADS), f32),
        'q_norm_g': gain(ks[14], (DEPTH, FOX_HEAD_DIM)),
        'k_norm_g': gain(ks[15], (DEPTH, FOX_HEAD_DIM)),
        'w_br_ret': nrm(ks[16], (DEPTH, RET_V_W, D_MODEL), RET_V_W ** -0.5),
        'w_br_fox': nrm(ks[17], (DEPTH, FOX_W, D_MODEL), FOX_W ** -0.5),
        'w_out': nrm(ks[18], (DEPTH, D_MODEL, D_MODEL), D_MODEL ** -0.5),
        'norm_ffn2_g': gain(ks[19], (DEPTH, D_MODEL)),
        'ffn2_w_in': nrm(ks[20], (DEPTH, D_MODEL, 2 * D_FF), D_MODEL ** -0.5),
        'ffn2_w_out': nrm(ks[21], (DEPTH, D_FF, D_MODEL), D_FF ** -0.5),
        'norm_ple_g': gain(ks[22], (DEPTH, D_MODEL)),
        'w_ple': nrm(ks[23], (DEPTH, PLE_DIM, D_MODEL), PLE_DIM ** -0.5),
        'w_ple_gate': nrm(ks[24], (DEPTH, D_MODEL, D_MODEL), D_MODEL ** -0.5),
    }


def reference(x_prompt, x_sample, p_prompt, p_sample, state_ret, cache_fox_k, cache_fox_v, cache_fox_logf,
              norm_ffn1_g, ffn1_w_in, ffn1_w_out, norm_mix_g, w_in_mix, b_forget, q_norm_g, k_norm_g,
              w_br_ret, w_br_fox, w_out, norm_ffn2_g, ffn2_w_in, ffn2_w_out, norm_ple_g, w_ple, w_ple_gate):
    B, S, _ = x_prompt.shape
    DB, L, _ = x_sample.shape
    pos_prompt = jnp.arange(S, dtype=jnp.float32)
    pos_sample = PAST_LEN + jnp.arange(L, dtype=jnp.float32)
    xp, xs = x_prompt, x_sample
    rp_l, kp_l, vp_l, lp_l = [], [], [], []
    rs_l, ks_l, vs_l, ls_l = [], [], [], []
    for i in range(DEPTH):
        prm = {
            'norm_ffn1_g': norm_ffn1_g[i], 'ffn1_w_in': ffn1_w_in[i], 'ffn1_w_out': ffn1_w_out[i],
            'norm_mix_g': norm_mix_g[i], 'w_in_mix': w_in_mix[i], 'b_forget': b_forget[i],
            'q_norm_g': q_norm_g[i], 'k_norm_g': k_norm_g[i], 'w_br_ret': w_br_ret[i],
            'w_br_fox': w_br_fox[i], 'w_out': w_out[i], 'norm_ffn2_g': norm_ffn2_g[i],
            'ffn2_w_in': ffn2_w_in[i], 'ffn2_w_out': ffn2_w_out[i], 'norm_ple_g': norm_ple_g[i],
            'w_ple': w_ple[i], 'w_ple_gate': w_ple_gate[i],
        }
        s0_prompt = jnp.zeros((B, RET_HEADS, RET_KEY_DIM, RET_VAL_DIM), state_ret.dtype)
        xp, rp, kp, vp, lp = layer(xp, p_prompt[i], pos_prompt, s0_prompt, None, prm, CHUNK)
        xs, rs, ks_, vs, ls = layer(xs, p_sample[i], pos_sample, state_ret[i],
                                    (cache_fox_k[i], cache_fox_v[i], cache_fox_logf[i]), prm, L)
        rp_l.append(rp); kp_l.append(kp); vp_l.append(vp); lp_l.append(lp)
        rs_l.append(rs); ks_l.append(ks_); vs_l.append(vs); ls_l.append(ls)
    new_state_ret_prompt = jnp.stack(rp_l)
    new_fox_k_prompt = jnp.stack(kp_l)
    new_fox_v_prompt = jnp.stack(vp_l)
    new_fox_logf_prompt = jnp.stack(lp_l)
    new_state_ret_sample = jnp.stack(rs_l)
    new_fox_k_sample = jnp.stack(ks_l)
    new_fox_v_sample = jnp.stack(vs_l)
    new_fox_logf_sample = jnp.stack(ls_l)
    return (xp, xs, new_state_ret_prompt, new_fox_k_prompt, new_fox_v_prompt, new_fox_logf_prompt,
            new_state_ret_sample, new_fox_k_sample, new_fox_v_sample, new_fox_logf_sample)
```

```python
import functools
import math

import jax
import jax.numpy as jnp
import numpy as np
from jax import lax
from jax.experimental import pallas as pl
from jax.experimental.pallas import tpu as pltpu

F32 = jnp.float32
BF16 = jnp.bfloat16

EPS = 1e-6
ROPE_BASE = 10000.0
RET_CHUNK = 64
RET_HEADS = 4
RET_KEY_DIM = 128
RET_VAL_DIM = 256
FOX_HEADS = 8
FOX_HEAD_DIM = 64
N_BRANCH = 2
LANES = 128
NEG_BIG = -1e30
VMEM_LIMIT_BYTES = 58 * 1024 * 1024

RET_QK_W = RET_HEADS * RET_KEY_DIM
RET_V_W = RET_HEADS * RET_VAL_DIM
FOX_W = FOX_HEADS * FOX_HEAD_DIM

NT_DIMS = (((1,), (1,)), ((), ()))
TN_DIMS = (((0,), (0,)), ((), ()))


def _cparams(n_axes):
    return pltpu.CompilerParams(dimension_semantics=("arbitrary",) * n_axes,
                                vmem_limit_bytes=VMEM_LIMIT_BYTES)


def _resident(shape):
    nd = len(shape)
    return pl.BlockSpec(shape, lambda *_: (0,) * nd, pipeline_mode=pl.Buffered(1))


def _rms(x, g=None):
    y = x * lax.rsqrt(jnp.mean(x * x, axis=-1, keepdims=True) + EPS)
    return y if g is None else y * g


def _dot(a, b):
    return jnp.dot(a, b, preferred_element_type=F32)


def _split3(x):
    hi = x.astype(BF16)
    r1 = x - hi.astype(F32)
    mid = r1.astype(BF16)
    lo = (r1 - mid.astype(F32)).astype(BF16)
    return hi, mid, lo


def _swiglu(x, g_ref, win_ref, wout_ref, a_scr, ff_chunk):
    d_ff = wout_ref.shape[0]
    h = _rms(x, g_ref[...]).astype(BF16)
    for c in range(d_ff // ff_chunk):
        lo = c * ff_chunk
        g = _dot(h, win_ref[:, lo:lo + ff_chunk])
        u = _dot(h, win_ref[:, d_ff + lo:d_ff + lo + ff_chunk])
        a_scr[:, lo:lo + ff_chunk] = (g * jax.nn.sigmoid(g) * u).astype(BF16)
    return _dot(a_scr[...], wout_ref[...])


def _ffn1_kernel(x_ref, g_ref, win_ref, wout_ref, o_ref, a_scr, *, ff_chunk):
    x = x_ref[...]
    o_ref[...] = x + 0.5 * _swiglu(x, g_ref, win_ref, wout_ref, a_scr, ff_chunk)


def _ffn1(x, g, w_in, w_out, tm, ff_chunk):
    n, d = x.shape
    d_ff = w_out.shape[0]
    return pl.pallas_call(
        functools.partial(_ffn1_kernel, ff_chunk=ff_chunk),
        grid=(n // tm,),
        in_specs=[pl.BlockSpec((tm, d), lambda i: (i, 0)),
                  _resident(g.shape), _resident(w_in.shape), _resident(w_out.shape)],
        out_specs=pl.BlockSpec((tm, d), lambda i: (i, 0)),
        out_shape=jax.ShapeDtypeStruct((n, d), F32),
        scratch_shapes=[pltpu.VMEM((tm, d_ff), BF16)],
        compiler_params=_cparams(1),
        name="ffn1",
    )(x, g, w_in, w_out)


_O_RQ = 0
_O_RK = _O_RQ + RET_QK_W
_O_RV = _O_RK + RET_QK_W
_O_RG = _O_RV + RET_V_W
_O_FQ = _O_RG + RET_V_W
_O_FK = _O_FQ + FOX_W
_O_FV = _O_FK + FOX_W
_O_FF = _O_FV + FOX_W
_O_ZG = _O_FF + LANES


def _mix_in_kernel(x_ref, g_ref, w_ref, bf_ref, qg_ref, kg_ref, hsum_ref, freq_ref,
                   rq_ref, rk_ref, rv_ref, rg_ref, fq_ref, fk_ref, fv_ref, lf_ref, cc_ref, cr_ref, gt_ref,
                   cos_scr, sin_scr, tri_scr, carry_scr, *, tm, seg, pos0, c_unit):
    i = pl.program_id(0)
    n_pos_tiles = max(seg // tm, 1)
    d_model = x_ref.shape[1]

    @pl.when(i == 0)
    def _():
        r = lax.broadcasted_iota(jnp.int32, (tm, tm), 0)
        c = lax.broadcasted_iota(jnp.int32, (tm, tm), 1)
        keep = c <= r
        if seg < tm:
            keep = keep & ((r & -seg) == (c & -seg))
        tri_scr[...] = jnp.where(keep, 1.0, 0.0).astype(BF16)

    slot = i % n_pos_tiles

    @pl.when(i < n_pos_tiles)
    def _():
        row = lax.broadcasted_iota(jnp.int32, (tm, LANES), 0)
        row = row + slot * tm if seg >= tm else row & (seg - 1)
        pos = (pos0 + row).astype(F32)
        ang = pos * freq_ref[...]
        lane = lax.broadcasted_iota(jnp.int32, (tm, LANES), 1)
        cos_scr[pl.ds(pl.multiple_of(slot * tm, tm), tm), :] = jnp.cos(ang)
        sin_scr[pl.ds(pl.multiple_of(slot * tm, tm), tm), :] = jnp.where(lane < LANES // 2, -1.0, 1.0) * jnp.sin(ang)

    cos = cos_scr[pl.ds(pl.multiple_of(slot * tm, tm), tm), :]
    sin = sin_scr[pl.ds(pl.multiple_of(slot * tm, tm), tm), :]

    h = _rms(x_ref[...], g_ref[...]).astype(BF16)

    def rotary(z, scale):
        out = z * cos + pltpu.roll(z, LANES // 2, axis=1) * sin
        return out if scale is None else out * scale

    zq = _dot(h, w_ref[:, _O_RQ:_O_RQ + RET_QK_W])
    zk = _dot(h, w_ref[:, _O_RK:_O_RK + RET_QK_W])
    for hd in range(RET_HEADS):
        sl = slice(hd * RET_KEY_DIM, (hd + 1) * RET_KEY_DIM)
        rq_ref[:, sl] = rotary(zq[:, sl], None).astype(BF16)
        rk_ref[:, sl] = rotary(zk[:, sl], RET_KEY_DIM ** -0.5).astype(BF16)

    rv_ref[...] = _dot(h, w_ref[:, _O_RV:_O_RV + RET_V_W]).astype(BF16)
    zg_ret = _dot(h, w_ref[:, _O_RG:_O_RG + RET_V_W])
    rg_ref[...] = (zg_ret * jax.nn.sigmoid(zg_ret)).astype(BF16)

    def head_norm(z, gain):
        ss = _dot((z * z).astype(BF16), hsum_ref[...])
        return z * lax.rsqrt(ss * (1.0 / FOX_HEAD_DIM) + EPS) * gain

    zfq = _dot(h, w_ref[:, _O_FQ:_O_FQ + FOX_W])
    fq_ref[...] = (head_norm(zfq, qg_ref[...]) * (FOX_HEAD_DIM ** -0.5)).astype(BF16)
    zfk = _dot(h, w_ref[:, _O_FK:_O_FK + FOX_W])
    fk_ref[...] = head_norm(zfk, kg_ref[...])
    fv_ref[...] = _dot(h, w_ref[:, _O_FV:_O_FV + FOX_W])

    zf = _dot(h, w_ref[:, _O_FF:_O_FF + LANES]) + bf_ref[...]
    lane = lax.broadcasted_iota(jnp.int32, (tm, LANES), 1)
    lf = jnp.where(lane < FOX_HEADS, jnp.minimum(zf, 0.0) - jnp.log1p(jnp.exp(-jnp.abs(zf))), 0.0)
    lf_ref[...] = lf[:, :FOX_HEADS]
    tri = tri_scr[...]
    csum = sum(_dot(tri, part) for part in _split3(lf))
    if seg >= tm:
        seq_start = (i * tm) % seg == 0
        csum = csum + jnp.where(seq_start, 0.0, carry_scr[...])
        carry_scr[...] = csum[tm - 1:tm, :]
    cc_ref[...] = csum[:, :FOX_HEADS]
    er = lax.broadcasted_iota(jnp.int32, (LANES, LANES), 0)
    ec = lax.broadcasted_iota(jnp.int32, (LANES, LANES), 1)
    eye = jnp.where(er == ec, 1.0, 0.0).astype(BF16)
    csum_t = sum(lax.dot_general(eye, part, NT_DIMS, preferred_element_type=F32) for part in _split3(csum))
    for u in range(tm // c_unit):
        cr_ref[u] = csum_t[:FOX_HEADS, u * c_unit:(u + 1) * c_unit]

    for c in range(N_BRANCH):
        z = _dot(h, w_ref[:, _O_ZG + c * d_model:_O_ZG + (c + 1) * d_model])
        gt_ref[:, c * d_model:(c + 1) * d_model] = jax.nn.sigmoid(z).astype(BF16)


def _mix_in(x, g, w, b_pad, qg, kg, hsum, freq, *, tm, seg, pos0, c_unit):
    n, d = x.shape
    n_pos_tiles = max(seg // tm, 1)
    row = lambda w_: pl.BlockSpec((tm, w_), lambda i: (i, 0))
    out_shapes = [
        jax.ShapeDtypeStruct((n, RET_QK_W), BF16), jax.ShapeDtypeStruct((n, RET_QK_W), BF16),
        jax.ShapeDtypeStruct((n, RET_V_W), BF16), jax.ShapeDtypeStruct((n, RET_V_W), BF16),
        jax.ShapeDtypeStruct((n, FOX_W), BF16), jax.ShapeDtypeStruct((n, FOX_W), F32),
        jax.ShapeDtypeStruct((n, FOX_W), F32), jax.ShapeDtypeStruct((n, FOX_HEADS), F32),
        jax.ShapeDtypeStruct((n, FOX_HEADS), F32), jax.ShapeDtypeStruct((n // c_unit, FOX_HEADS, c_unit), F32),
        jax.ShapeDtypeStruct((n, N_BRANCH * d), BF16),
    ]
    out_specs = [row(RET_QK_W), row(RET_QK_W), row(RET_V_W), row(RET_V_W), row(FOX_W), row(FOX_W), row(FOX_W),
                 row(FOX_HEADS), row(FOX_HEADS),
                 pl.BlockSpec((tm // c_unit, FOX_HEADS, c_unit), lambda i: (i, 0, 0)),
                 row(N_BRANCH * d)]
    return pl.pallas_call(
        functools.partial(_mix_in_kernel, tm=tm, seg=seg, pos0=pos0, c_unit=c_unit),
        grid=(n // tm,),
        in_specs=[row(d), _resident(g.shape), _resident(w.shape), _resident(b_pad.shape), _resident(qg.shape),
                  _resident(kg.shape), _resident(hsum.shape), _resident(freq.shape)],
        out_specs=out_specs,
        out_shape=out_shapes,
        scratch_shapes=[pltpu.VMEM((n_pos_tiles * tm, LANES), F32), pltpu.VMEM((n_pos_tiles * tm, LANES), F32),
                        pltpu.VMEM((tm, tm), BF16), pltpu.VMEM((1, LANES), F32)],
        compiler_params=_cparams(1),
        name="mix_in",
    )(x, g, w, b_pad, qg, kg, hsum, freq)


def _retention_kernel(*refs, t_blk, chunk, has_s0):
    if has_s0:
        q_ref, k_ref, v_ref, rg_ref, s0_ref, og_ref, st_ref, s_scr, m_scr = refs
    else:
        q_ref, k_ref, v_ref, rg_ref, og_ref, st_ref, s_scr, m_scr = refs
        s0_ref = None
    b, t = pl.program_id(0), pl.program_id(1)
    n_t = pl.num_programs(1)
    log_gamma = [math.log(1.0 - 2.0 ** (-5.0 - hd)) for hd in range(RET_HEADS)]
    shift = chunk.bit_length() - 1

    @pl.when((b == 0) & (t == 0))
    def _():
        r = lax.broadcasted_iota(jnp.int32, (t_blk, t_blk), 0)
        c = lax.broadcasted_iota(jnp.int32, (t_blk, t_blk), 1)
        keep = ((r >> shift) == (c >> shift)) | (c < r)
        dist = jnp.abs(r - c).astype(F32)
        for hd in range(RET_HEADS):
            m_scr[hd] = jnp.where(keep, jnp.exp(log_gamma[hd] * dist), 0.0)

    @pl.when(t == 0)
    def _():
        if has_s0:
            s_scr[...] = s0_ref[0]
        else:
            s_scr[...] = jnp.zeros_like(s_scr)

    idx = lax.broadcasted_iota(jnp.int32, (t_blk, 1), 0).astype(F32)
    for hd in range(RET_HEADS):
        lg = log_gamma[hd]
        qh = q_ref[:, hd * RET_KEY_DIM:(hd + 1) * RET_KEY_DIM]
        kh = k_ref[:, hd * RET_KEY_DIM:(hd + 1) * RET_KEY_DIM]
        vh = v_ref[:, hd * RET_VAL_DIM:(hd + 1) * RET_VAL_DIM]
        state = s_scr[hd]
        att = lax.dot_general(qh, kh, NT_DIMS, preferred_element_type=F32) * m_scr[hd]
        o = _dot(att.astype(BF16), vh) + _dot(qh, state.astype(BF16)) * jnp.exp(lg * (idx + 1.0))
        kd = (kh.astype(F32) * jnp.exp(lg * (t_blk - 1.0 - idx))).astype(BF16)
        new_state = state * math.exp(lg * t_blk) + lax.dot_general(kd, vh, TN_DIMS, preferred_element_type=F32)
        s_scr[hd] = new_state
        gate = rg_ref[:, hd * RET_VAL_DIM:(hd + 1) * RET_VAL_DIM].astype(F32)
        og_ref[:, hd * RET_VAL_DIM:(hd + 1) * RET_VAL_DIM] = (_rms(o) * gate).astype(BF16)

        @pl.when(t == n_t - 1)
        def _():
            st_ref[0, hd] = new_state


def _retention(q, k, v, rg, s0, *, n_seq, t_blk, chunk):
    n = q.shape[0]
    n_t = n // n_seq // t_blk
    row = lambda w_: pl.BlockSpec((t_blk, w_), lambda b, t: (b * n_t + t, 0))
    st_shape = (n_seq, RET_HEADS, RET_KEY_DIM, RET_VAL_DIM)
    st_spec = pl.BlockSpec((1,) + st_shape[1:], lambda b, t: (b, 0, 0, 0))
    in_specs = [row(RET_QK_W), row(RET_QK_W), row(RET_V_W), row(RET_V_W)]
    args = [q, k, v, rg]
    if s0 is not None:
        in_specs.append(st_spec)
        args.append(s0)
    return pl.pallas_call(
        functools.partial(_retention_kernel, t_blk=t_blk, chunk=chunk, has_s0=s0 is not None),
        grid=(n_seq, n_t),
        in_specs=in_specs,
        out_specs=[row(RET_V_W), st_spec],
        out_shape=[jax.ShapeDtypeStruct((n, RET_V_W), BF16), jax.ShapeDtypeStruct(st_shape, F32)],
        scratch_shapes=[pltpu.VMEM(st_shape[1:], F32), pltpu.VMEM((RET_HEADS, t_blk, t_blk), F32)],
        compiler_params=_cparams(2),
        name="retention",
    )(*args)


def _flash_step(carry, qm, kb, vb, ci, cj, causal):
    m, l, acc = carry
    s = lax.dot_general(qm, kb, NT_DIMS, preferred_element_type=F32) + (ci - cj)
    if causal:
        r = lax.broadcasted_iota(jnp.int32, s.shape, 0)
        c = lax.broadcasted_iota(jnp.int32, s.shape, 1)
        s = jnp.where(c <= r, s, NEG_BIG)
    m_new = jnp.maximum(m, jnp.max(s, axis=-1, keepdims=True))
    alpha = jnp.exp(m - m_new)
    p = jnp.exp(s - m_new)
    l = alpha * l + jnp.sum(p, axis=-1, keepdims=True)
    acc = alpha * acc + _dot(p.astype(BF16), vb)
    return m_new, l, acc


def _flash_init(tq):
    return (jnp.full((tq, 1), NEG_BIG, F32), jnp.zeros((tq, 1), F32), jnp.zeros((tq, LANES), F32))


def _head_pairs(q_ref, o_ref, attend):
    tq = q_ref.shape[0]
    lane = lax.broadcasted_iota(jnp.int32, (tq, LANES), 1)
    for pair in range(FOX_HEADS // 2):
        sl = slice(pair * LANES, (pair + 1) * LANES)
        q2 = q_ref[:, sl]
        outs = []
        for e in range(2):
            in_head = (lane < FOX_HEAD_DIM) if e == 0 else (lane >= FOX_HEAD_DIM)
            qm = jnp.where(in_head, q2, jnp.zeros_like(q2))
            _, l, acc = attend(2 * pair + e, sl, qm)
            outs.append(acc / l)
        o_ref[:, sl] = jnp.where(lane < FOX_HEAD_DIM, outs[0], outs[1]).astype(BF16)


def _fox_prompt_kernel(q_ref, k_ref, v_ref, cc_ref, cr_ref, o_ref, *, tq):
    qi = pl.program_id(1)

    def attend(hd, sl, qm):
        ci = cc_ref[:, hd:hd + 1]

        def step(j, carry, causal):
            rows = pl.ds(pl.multiple_of(j * tq, tq), tq)
            kb = k_ref[rows, sl].astype(BF16)
            vb = v_ref[rows, sl].astype(BF16)
            cj = cr_ref[j][hd:hd + 1, :]
            return _flash_step(carry, qm, kb, vb, ci, cj, causal)

        carry = lax.fori_loop(0, qi, lambda j, c: step(j, c, False), _flash_init(tq))
        return step(qi, carry, True)

    _head_pairs(q_ref, o_ref, attend)


def _fox_prompt(fq, fk, fv, ccol, crow, *, n_seq, tq):
    n = fq.shape[0]
    s_len = n // n_seq
    n_q = s_len // tq
    return pl.pallas_call(
        functools.partial(_fox_prompt_kernel, tq=tq),
        grid=(n_seq, n_q),
        in_specs=[pl.BlockSpec((tq, FOX_W), lambda b, i: (b * n_q + i, 0)),
                  pl.BlockSpec((s_len, FOX_W), lambda b, i: (b, 0)),
                  pl.BlockSpec((s_len, FOX_W), lambda b, i: (b, 0)),
                  pl.BlockSpec((tq, FOX_HEADS), lambda b, i: (b * n_q + i, 0)),
                  pl.BlockSpec((n_q, FOX_HEADS, tq), lambda b, i: (b, 0, 0))],
        out_specs=pl.BlockSpec((tq, FOX_W), lambda b, i: (b * n_q + i, 0)),
        out_shape=jax.ShapeDtypeStruct((n, FOX_W), BF16),
        compiler_params=_cparams(2),
        name="fox_prompt",
    )(fq, fk, fv, ccol, crow)


def _fox_sample_kernel(q_ref, ck_ref, cv_ref, clf_ref, k_ref, v_ref, cc_ref, cr_ref, o_ref, cpast_scr, *, tk):
    tq = q_ref.shape[0]
    p_len = ck_ref.shape[0]

    x = clf_ref[0]
    lane = lax.broadcasted_iota(jnp.int32, x.shape, 1)
    sh = 1
    while sh < p_len:
        x = x + jnp.where(lane >= sh, pltpu.roll(x, sh, axis=1), 0.0)
        sh *= 2
    cpast_scr[...] = x - x[:, p_len - 1:p_len]

    def attend(hd, sl, qm):
        ci = cc_ref[:, hd:hd + 1]
        carry = _flash_init(tq)
        for j in range(p_len // tk):
            rows = slice(j * tk, (j + 1) * tk)
            carry = _flash_step(carry, qm, ck_ref[rows, sl].astype(BF16), cv_ref[rows, sl].astype(BF16),
                                ci, cpast_scr[hd:hd + 1, rows], False)
        return _flash_step(carry, qm, k_ref[:, sl].astype(BF16), v_ref[:, sl].astype(BF16),
                           ci, cr_ref[0][hd:hd + 1, :], True)

    _head_pairs(q_ref, o_ref, attend)


def _fox_sample(fq, ck, cv, clf_t, fk, fv, ccol, crow, *, n_seq, tk):
    n = fq.shape[0]
    l_new = n // n_seq
    p_len = ck.shape[0] // n_seq
    new = lambda w_: pl.BlockSpec((l_new, w_), lambda b: (b, 0))
    return pl.pallas_call(
        functools.partial(_fox_sample_kernel, tk=tk),
        grid=(n_seq,),
        in_specs=[new(FOX_W),
                  pl.BlockSpec((p_len, FOX_W), lambda b: (b, 0)),
                  pl.BlockSpec((p_len, FOX_W), lambda b: (b, 0)),
                  pl.BlockSpec((1, FOX_HEADS, p_len), lambda b: (b, 0, 0)),
                  new(FOX_W), new(FOX_W), new(FOX_HEADS),
                  pl.BlockSpec((1, FOX_HEADS, l_new), lambda b: (b, 0, 0))],
        out_specs=new(FOX_W),
        out_shape=jax.ShapeDtypeStruct((n, FOX_W), BF16),
        scratch_shapes=[pltpu.VMEM((FOX_HEADS, p_len), F32)],
        compiler_params=_cparams(1),
        name="fox_sample",
    )(fq, ck, cv, clf_t, fk, fv, ccol, crow)


def _post_kernel(x_ref, og_ref, of_ref, gt_ref, p_ref, wbr_ref, wbf_ref, wo_ref, g2_ref, win_ref, wout_ref,
                 gp_ref, wpg_ref, wple_ref, o_ref, a_scr, *, ff_chunk):
    d = x_ref.shape[1]
    br_ret = _dot(og_ref[...], wbr_ref[...])
    br_fox = _dot(of_ref[...], wbf_ref[...])
    merged = gt_ref[:, :d].astype(F32) * br_ret + gt_ref[:, d:].astype(F32) * br_fox
    x = x_ref[...] + _dot(merged.astype(BF16), wo_ref[...])
    x = x + 0.5 * _swiglu(x, g2_ref, win_ref, wout_ref, a_scr, ff_chunk)
    gate = jax.nn.sigmoid(_dot(_rms(x, gp_ref[...]).astype(BF16), wpg_ref[...]))
    o_ref[...] = x + _dot(p_ref[...].astype(BF16), wple_ref[...]) * gate


def _post(x, og, of, gates, p, wbr, wbf, wo, g2, w_in, w_out, gp, wpg, wple, *, tm, ff_chunk):
    n, d = x.shape
    d_ff = w_out.shape[0]
    row = lambda w_: pl.BlockSpec((tm, w_), lambda i: (i, 0))
    weights = [wbr, wbf, wo, g2, w_in, w_out, gp, wpg, wple]
    return pl.pallas_call(
        functools.partial(_post_kernel, ff_chunk=ff_chunk),
        grid=(n // tm,),
        in_specs=[row(d), row(og.shape[1]), row(of.shape[1]), row(gates.shape[1]), row(p.shape[1])]
        + [_resident(w.shape) for w in weights],
        out_specs=row(d),
        out_shape=jax.ShapeDtypeStruct((n, d), F32),
        scratch_shapes=[pltpu.VMEM((tm, d_ff), BF16)],
        compiler_params=_cparams(1),
        name="post",
    )(x, og, of, gates, p, *weights)


def _row_tile(n, seg, want):
    t = want
    while t > 8 and (n % t or (seg % t and t % seg)):
        t //= 2
    assert n % t == 0 and (seg % t == 0 or (t % seg == 0 and seg & (seg - 1) == 0)), (n, seg, t)
    return t


def _layer(x, p, s0, fox_cache, prm, *, ret_chunk, pos0):
    b, s, d = x.shape
    n = b * s
    xf = x.reshape(n, d)
    tm = _row_tile(n, s, 512)
    ff_chunk = 256

    x1 = _ffn1(xf, prm["g1"], prm["w1_in"], prm["w1_out"], tm, ff_chunk)

    is_prompt = fox_cache is None
    tq = 256 if is_prompt else s
    assert s % tq == 0 and tm % tq == 0
    rq, rk, rv, rg, fq, fk, fv, logf, ccol, crow, gates = _mix_in(
        x1, prm["gm"], prm["w_mix"], prm["b_forget"], prm["qg"], prm["kg"], prm["hsum"], prm["freq"],
        tm=tm, seg=s, pos0=pos0, c_unit=tq)

    t_blk = math.gcd(s, 256)
    assert t_blk % ret_chunk == 0 or ret_chunk % t_blk == 0
    og, state = _retention(rq, rk, rv, rg, s0, n_seq=b, t_blk=t_blk, chunk=min(ret_chunk, t_blk))

    if is_prompt:
        of = _fox_prompt(fq, fk, fv, ccol, crow, n_seq=b, tq=tq)
    else:
        ck, cv, clf = fox_cache
        p_len = ck.shape[1]
        of = _fox_sample(fq, ck.reshape(b * p_len, FOX_W), cv.reshape(b * p_len, FOX_W),
                         jnp.swapaxes(clf, 1, 2), fk, fv, ccol, crow, n_seq=b, tk=math.gcd(p_len, 1024))

    y = _post(x1, og, of, gates, p.reshape(n, -1), prm["w_br_ret"], prm["w_br_fox"], prm["w_out"], prm["g2"],
              prm["w2_in"], prm["w2_out"], prm["gp"], prm["w_ple_gate"], prm["w_ple"], tm=tm, ff_chunk=ff_chunk)
    return (y.reshape(b, s, d), state, fk.reshape(b, s, FOX_HEADS, FOX_HEAD_DIM),
            fv.reshape(b, s, FOX_HEADS, FOX_HEAD_DIM), logf.reshape(b, s, FOX_HEADS))


def _prep_params(i, norm_ffn1_g, ffn1_w_in, ffn1_w_out, norm_mix_g, w_in_mix, b_forget, q_norm_g, k_norm_g,
                 w_br_ret, w_br_fox, w_out, norm_ffn2_g, ffn2_w_in, ffn2_w_out, norm_ple_g, w_ple, w_ple_gate):
    wm = w_in_mix[i]
    o_ff = 2 * RET_QK_W + 2 * RET_V_W + 3 * FOX_W
    w_mix = jnp.concatenate(
        [wm[:, :o_ff], jnp.pad(wm[:, o_ff:o_ff + FOX_HEADS], ((0, 0), (0, LANES - FOX_HEADS))),
         wm[:, o_ff + FOX_HEADS:]], axis=1).astype(BF16)
    head_of = np.arange(FOX_W) // FOX_HEAD_DIM
    half = RET_KEY_DIM // 2
    freqs = ROPE_BASE ** (-jnp.arange(half, dtype=F32) / half)
    return {
        "g1": norm_ffn1_g[i][None, :], "w1_in": ffn1_w_in[i].astype(BF16), "w1_out": ffn1_w_out[i].astype(BF16),
        "gm": norm_mix_g[i][None, :], "w_mix": w_mix,
        "b_forget": jnp.pad(b_forget[i], (0, LANES - FOX_HEADS))[None, :].astype(F32),
        "qg": jnp.tile(q_norm_g[i], FOX_HEADS)[None, :], "kg": jnp.tile(k_norm_g[i], FOX_HEADS)[None, :],
        "hsum": jnp.asarray(head_of[:, None] == head_of[None, :], BF16),
        "freq": jnp.concatenate([freqs, freqs])[None, :],
        "w_br_ret": w_br_ret[i].astype(BF16), "w_br_fox": w_br_fox[i].astype(BF16), "w_out": w_out[i].astype(BF16),
        "g2": norm_ffn2_g[i][None, :], "w2_in": ffn2_w_in[i].astype(BF16), "w2_out": ffn2_w_out[i].astype(BF16),
        "gp": norm_ple_g[i][None, :], "w_ple_gate": w_ple_gate[i].astype(BF16), "w_ple": w_ple[i].astype(BF16),
    }


def kernel(x_prompt, x_sample, p_prompt, p_sample, state_ret, cache_fox_k, cache_fox_v, cache_fox_logf, norm_ffn1_g, ffn1_w_in, ffn1_w_out, norm_mix_g, w_in_mix, b_forget, q_norm_g, k_norm_g, w_br_ret, w_br_fox, w_out, norm_ffn2_g, ffn2_w_in, ffn2_w_out, norm_ple_g, w_ple, w_ple_gate):
    depth = p_prompt.shape[0]
    past_len = cache_fox_k.shape[2]
    dec_seq = x_sample.shape[1]
    xp, xs = x_prompt, x_sample
    outs_p, outs_s = [], []
    for i in range(depth):
        prm = _prep_params(i, norm_ffn1_g, ffn1_w_in, ffn1_w_out, norm_mix_g, w_in_mix, b_forget, q_norm_g,
                           k_norm_g, w_br_ret, w_br_fox, w_out, norm_ffn2_g, ffn2_w_in, ffn2_w_out, norm_ple_g,
                           w_ple, w_ple_gate)
        xp, *rest_p = _layer(xp, p_prompt[i], None, None, prm, ret_chunk=RET_CHUNK, pos0=0)
        xs, *rest_s = _layer(xs, p_sample[i], state_ret[i],
                             (cache_fox_k[i], cache_fox_v[i], cache_fox_logf[i]), prm,
                             ret_chunk=dec_seq, pos0=past_len)
        outs_p.append(rest_p)
        outs_s.append(rest_s)
    stack = lambda outs, j: jnp.stack([o[j] for o in outs])
    return (xp, xs, stack(outs_p, 0), stack(outs_p, 1), stack(outs_p, 2), stack(outs_p, 3),
            stack(outs_s, 0), stack(outs_s, 1), stack(outs_s, 2), stack(outs_s, 3))
```

```python
import functools
import math

import jax
import jax.numpy as jnp
import numpy as np
from jax import lax
from jax.experimental import pallas as pl
from jax.experimental.pallas import tpu as pltpu

F32 = jnp.float32
BF16 = jnp.bfloat16

EPS = 1e-6
ROPE_BASE = 10000.0
RET_CHUNK = 64
RET_HEADS = 4
RET_KEY_DIM = 128
RET_VAL_DIM = 256
FOX_HEADS = 8
FOX_HEAD_DIM = 64
N_BRANCH = 2
FOX_LOOKAHEAD = 5
LANES = 128
NEG_BIG = -1e30
VMEM_LIMIT_BYTES = 58 * 1024 * 1024

RET_QK_W = RET_HEADS * RET_KEY_DIM
RET_V_W = RET_HEADS * RET_VAL_DIM
FOX_W = FOX_HEADS * FOX_HEAD_DIM

NT_DIMS = (((1,), (1,)), ((), ()))
TN_DIMS = (((0,), (0,)), ((), ()))


def _cparams(n_axes):
    return pltpu.CompilerParams(dimension_semantics=("arbitrary",) * n_axes,
                                vmem_limit_bytes=VMEM_LIMIT_BYTES)


def _resident(shape):
    nd = len(shape)
    return pl.BlockSpec(shape, lambda *_: (0,) * nd, pipeline_mode=pl.Buffered(1))


def _rms(x, g=None):
    y = x * lax.rsqrt(jnp.mean(x * x, axis=-1, keepdims=True) + EPS)
    return y if g is None else y * g


def _dot(a, b):
    return jnp.dot(a, b, preferred_element_type=F32)


def _split3(x):
    hi = x.astype(BF16)
    r1 = x - hi.astype(F32)
    mid = r1.astype(BF16)
    lo = (r1 - mid.astype(F32)).astype(BF16)
    return hi, mid, lo


def _swiglu(x, g_ref, win_ref, wout_ref, a_scr, ff_chunk):
    d_ff = wout_ref.shape[0]
    h = _rms(x, g_ref[...]).astype(BF16)
    for c in range(d_ff // ff_chunk):
        lo = c * ff_chunk
        g = _dot(h, win_ref[:, lo:lo + ff_chunk])
        u = _dot(h, win_ref[:, d_ff + lo:d_ff + lo + ff_chunk])
        a_scr[:, lo:lo + ff_chunk] = (g * jax.nn.sigmoid(g) * u).astype(BF16)
    return _dot(a_scr[...], wout_ref[...])


def _ffn1_kernel(x_ref, g_ref, win_ref, wout_ref, o_ref, a_scr, *, ff_chunk):
    x = x_ref[...]
    o_ref[...] = x + 0.5 * _swiglu(x, g_ref, win_ref, wout_ref, a_scr, ff_chunk)


def _ffn1(x, g, w_in, w_out, tm, ff_chunk):
    n, d = x.shape
    d_ff = w_out.shape[0]
    return pl.pallas_call(
        functools.partial(_ffn1_kernel, ff_chunk=ff_chunk),
        grid=(n // tm,),
        in_specs=[pl.BlockSpec((tm, d), lambda i: (i, 0)),
                  _resident(g.shape), _resident(w_in.shape), _resident(w_out.shape)],
        out_specs=pl.BlockSpec((tm, d), lambda i: (i, 0)),
        out_shape=jax.ShapeDtypeStruct((n, d), F32),
        scratch_shapes=[pltpu.VMEM((tm, d_ff), BF16)],
        compiler_params=_cparams(1),
        name="ffn1",
    )(x, g, w_in, w_out)


_O_RQ = 0
_O_RK = _O_RQ + RET_QK_W
_O_RV = _O_RK + RET_QK_W
_O_RG = _O_RV + RET_V_W
_O_FQ = _O_RG + RET_V_W
_O_FK = _O_FQ + FOX_W
_O_FV = _O_FK + FOX_W
_O_FF = _O_FV + FOX_W
_O_ZG = _O_FF + LANES


def _mix_in_kernel(x_ref, g_ref, w_ref, bf_ref, qg_ref, kg_ref, hsum_ref, freq_ref,
                   rq_ref, rk_ref, rv_ref, rg_ref, fq_ref, fk_ref, fv_ref, lf_ref, cc_ref, cr_ref, gt_ref,
                   cos_scr, sin_scr, tri_scr, carry_scr, *, tm, seg, pos0, c_unit):
    i = pl.program_id(0)
    n_pos_tiles = max(seg // tm, 1)
    d_model = x_ref.shape[1]

    @pl.when(i == 0)
    def _():
        r = lax.broadcasted_iota(jnp.int32, (tm, tm), 0)
        c = lax.broadcasted_iota(jnp.int32, (tm, tm), 1)
        keep = c <= r
        if seg < tm:
            keep = keep & ((r & -seg) == (c & -seg))
        tri_scr[...] = jnp.where(keep, 1.0, 0.0).astype(BF16)

    slot = i % n_pos_tiles

    @pl.when(i < n_pos_tiles)
    def _():
        row = lax.broadcasted_iota(jnp.int32, (tm, LANES), 0)
        row = row + slot * tm if seg >= tm else row & (seg - 1)
        pos = (pos0 + row).astype(F32)
        ang = pos * freq_ref[...]
        lane = lax.broadcasted_iota(jnp.int32, (tm, LANES), 1)
        cos_scr[pl.ds(pl.multiple_of(slot * tm, tm), tm), :] = jnp.cos(ang)
        sin_scr[pl.ds(pl.multiple_of(slot * tm, tm), tm), :] = jnp.where(lane < LANES // 2, -1.0, 1.0) * jnp.sin(ang)

    cos = cos_scr[pl.ds(pl.multiple_of(slot * tm, tm), tm), :]
    sin = sin_scr[pl.ds(pl.multiple_of(slot * tm, tm), tm), :]

    h = _rms(x_ref[...], g_ref[...]).astype(BF16)

    def rotary(z, scale):
        out = z * cos + pltpu.roll(z, LANES // 2, axis=1) * sin
        return out if scale is None else out * scale

    zq = _dot(h, w_ref[:, _O_RQ:_O_RQ + RET_QK_W])
    zk = _dot(h, w_ref[:, _O_RK:_O_RK + RET_QK_W])
    for hd in range(RET_HEADS):
        sl = slice(hd * RET_KEY_DIM, (hd + 1) * RET_KEY_DIM)
        rq_ref[:, sl] = rotary(zq[:, sl], None).astype(BF16)
        rk_ref[:, sl] = rotary(zk[:, sl], RET_KEY_DIM ** -0.5).astype(BF16)

    rv_ref[...] = _dot(h, w_ref[:, _O_RV:_O_RV + RET_V_W]).astype(BF16)
    zg_ret = _dot(h, w_ref[:, _O_RG:_O_RG + RET_V_W])
    rg_ref[...] = (zg_ret * jax.nn.sigmoid(zg_ret)).astype(BF16)

    def head_norm(z, gain):
        ss = _dot((z * z).astype(BF16), hsum_ref[...])
        return z * lax.rsqrt(ss * (1.0 / FOX_HEAD_DIM) + EPS) * gain

    zfq = _dot(h, w_ref[:, _O_FQ:_O_FQ + FOX_W])
    fq_ref[...] = (head_norm(zfq, qg_ref[...]) * (FOX_HEAD_DIM ** -0.5)).astype(BF16)
    zfk = _dot(h, w_ref[:, _O_FK:_O_FK + FOX_W])
    fk_ref[...] = head_norm(zfk, kg_ref[...])
    fv_ref[...] = _dot(h, w_ref[:, _O_FV:_O_FV + FOX_W])

    zf = _dot(h, w_ref[:, _O_FF:_O_FF + LANES]) + bf_ref[...]
    lane = lax.broadcasted_iota(jnp.int32, (tm, LANES), 1)
    lf = jnp.where(lane < FOX_HEADS, jnp.minimum(zf, 0.0) - jnp.log1p(jnp.exp(-jnp.abs(zf))), 0.0)
    lf_ref[...] = lf[:, :FOX_HEADS]
    tri = tri_scr[...]
    csum = sum(_dot(tri, part) for part in _split3(lf))
    if seg >= tm:
        seq_start = (i * tm) % seg == 0
        csum = csum + jnp.where(seq_start, 0.0, carry_scr[...])
        carry_scr[...] = csum[tm - 1:tm, :]
    cc_ref[...] = csum[:, :FOX_HEADS]
    er = lax.broadcasted_iota(jnp.int32, (LANES, LANES), 0)
    ec = lax.broadcasted_iota(jnp.int32, (LANES, LANES), 1)
    eye = jnp.where(er == ec, 1.0, 0.0).astype(BF16)
    csum_t = sum(lax.dot_general(eye, part, NT_DIMS, preferred_element_type=F32) for part in _split3(csum))
    for u in range(tm // c_unit):
        cr_ref[u] = csum_t[:FOX_HEADS, u * c_unit:(u + 1) * c_unit]

    for c in range(N_BRANCH):
        z = _dot(h, w_ref[:, _O_ZG + c * d_model:_O_ZG + (c + 1) * d_model])
        gt_ref[:, c * d_model:(c + 1) * d_model] = jax.nn.sigmoid(z).astype(BF16)


def _mix_in(x, g, w, b_pad, qg, kg, hsum, freq, *, tm, seg, pos0, c_unit):
    n, d = x.shape
    n_pos_tiles = max(seg // tm, 1)
    row = lambda w_: pl.BlockSpec((tm, w_), lambda i: (i, 0))
    out_shapes = [
        jax.ShapeDtypeStruct((n, RET_QK_W), BF16), jax.ShapeDtypeStruct((n, RET_QK_W), BF16),
        jax.ShapeDtypeStruct((n, RET_V_W), BF16), jax.ShapeDtypeStruct((n, RET_V_W), BF16),
        jax.ShapeDtypeStruct((n, FOX_W), BF16), jax.ShapeDtypeStruct((n, FOX_W), F32),
        jax.ShapeDtypeStruct((n, FOX_W), F32), jax.ShapeDtypeStruct((n, FOX_HEADS), F32),
        jax.ShapeDtypeStruct((n, FOX_HEADS), F32), jax.ShapeDtypeStruct((n // c_unit, FOX_HEADS, c_unit), F32),
        jax.ShapeDtypeStruct((n, N_BRANCH * d), BF16),
    ]
    out_specs = [row(RET_QK_W), row(RET_QK_W), row(RET_V_W), row(RET_V_W), row(FOX_W), row(FOX_W), row(FOX_W),
                 row(FOX_HEADS), row(FOX_HEADS),
                 pl.BlockSpec((tm // c_unit, FOX_HEADS, c_unit), lambda i: (i, 0, 0)),
                 row(N_BRANCH * d)]
    return pl.pallas_call(
        functools.partial(_mix_in_kernel, tm=tm, seg=seg, pos0=pos0, c_unit=c_unit),
        grid=(n // tm,),
        in_specs=[row(d), _resident(g.shape), _resident(w.shape), _resident(b_pad.shape), _resident(qg.shape),
                  _resident(kg.shape), _resident(hsum.shape), _resident(freq.shape)],
        out_specs=out_specs,
        out_shape=out_shapes,
        scratch_shapes=[pltpu.VMEM((n_pos_tiles * tm, LANES), F32), pltpu.VMEM((n_pos_tiles * tm, LANES), F32),
                        pltpu.VMEM((tm, tm), BF16), pltpu.VMEM((1, LANES), F32)],
        compiler_params=_cparams(1),
        name="mix_in",
    )(x, g, w, b_pad, qg, kg, hsum, freq)


def _retention_kernel(*refs, t_blk, chunk, has_s0):
    if has_s0:
        q_ref, k_ref, v_ref, rg_ref, s0_ref, og_ref, st_ref, s_scr, m_scr = refs
    else:
        q_ref, k_ref, v_ref, rg_ref, og_ref, st_ref, s_scr, m_scr = refs
        s0_ref = None
    b, t = pl.program_id(0), pl.program_id(1)
    n_t = pl.num_programs(1)
    log_gamma = [math.log(1.0 - 2.0 ** (-5.0 - hd)) for hd in range(RET_HEADS)]
    shift = chunk.bit_length() - 1

    @pl.when((b == 0) & (t == 0))
    def _():
        r = lax.broadcasted_iota(jnp.int32, (t_blk, t_blk), 0)
        c = lax.broadcasted_iota(jnp.int32, (t_blk, t_blk), 1)
        keep = ((r >> shift) == (c >> shift)) | (c < r)
        dist = jnp.abs(r - c).astype(F32)
        for hd in range(RET_HEADS):
            m_scr[hd] = jnp.where(keep, jnp.exp(log_gamma[hd] * dist), 0.0)

    @pl.when(t == 0)
    def _():
        if has_s0:
            s_scr[...] = s0_ref[0]
        else:
            s_scr[...] = jnp.zeros_like(s_scr)

    idx = lax.broadcasted_iota(jnp.int32, (t_blk, 1), 0).astype(F32)
    for hd in range(RET_HEADS):
        lg = log_gamma[hd]
        qh = q_ref[:, hd * RET_KEY_DIM:(hd + 1) * RET_KEY_DIM]
        kh = k_ref[:, hd * RET_KEY_DIM:(hd + 1) * RET_KEY_DIM]
        vh = v_ref[:, hd * RET_VAL_DIM:(hd + 1) * RET_VAL_DIM]
        state = s_scr[hd]
        att = lax.dot_general(qh, kh, NT_DIMS, preferred_element_type=F32) * m_scr[hd]
        o = _dot(att.astype(BF16), vh) + _dot(qh, state.astype(BF16)) * jnp.exp(lg * (idx + 1.0))
        kd = (kh.astype(F32) * jnp.exp(lg * (t_blk - 1.0 - idx))).astype(BF16)
        new_state = state * math.exp(lg * t_blk) + lax.dot_general(kd, vh, TN_DIMS, preferred_element_type=F32)
        s_scr[hd] = new_state
        gate = rg_ref[:, hd * RET_VAL_DIM:(hd + 1) * RET_VAL_DIM].astype(F32)
        og_ref[:, hd * RET_VAL_DIM:(hd + 1) * RET_VAL_DIM] = (_rms(o) * gate).astype(BF16)

        @pl.when(t == n_t - 1)
        def _():
            st_ref[0, hd] = new_state


def _retention(q, k, v, rg, s0, *, n_seq, t_blk, chunk):
    n = q.shape[0]
    n_t = n // n_seq // t_blk
    row = lambda w_: pl.BlockSpec((t_blk, w_), lambda b, t: (b * n_t + t, 0))
    st_shape = (n_seq, RET_HEADS, RET_KEY_DIM, RET_VAL_DIM)
    st_spec = pl.BlockSpec((1,) + st_shape[1:], lambda b, t: (b, 0, 0, 0))
    in_specs = [row(RET_QK_W), row(RET_QK_W), row(RET_V_W), row(RET_V_W)]
    args = [q, k, v, rg]
    if s0 is not None:
        in_specs.append(st_spec)
        args.append(s0)
    return pl.pallas_call(
        functools.partial(_retention_kernel, t_blk=t_blk, chunk=chunk, has_s0=s0 is not None),
        grid=(n_seq, n_t),
        in_specs=in_specs,
        out_specs=[row(RET_V_W), st_spec],
        out_shape=[jax.ShapeDtypeStruct((n, RET_V_W), BF16), jax.ShapeDtypeStruct(st_shape, F32)],
        scratch_shapes=[pltpu.VMEM(st_shape[1:], F32), pltpu.VMEM((RET_HEADS, t_blk, t_blk), F32)],
        compiler_params=_cparams(2),
        name="retention",
    )(*args)


def _flash_step(carry, qm, kb, vb, ci, cj, causal):
    m, l, acc = carry
    s = lax.dot_general(qm, kb, NT_DIMS, preferred_element_type=F32) + (ci - cj)
    if causal:
        r = lax.broadcasted_iota(jnp.int32, s.shape, 0)
        c = lax.broadcasted_iota(jnp.int32, s.shape, 1)
        s = jnp.where(c <= r, s, NEG_BIG)
    m_new = jnp.maximum(m, jnp.max(s, axis=-1, keepdims=True))
    alpha = jnp.exp(m - m_new)
    p = jnp.exp(s - m_new)
    l = alpha * l + jnp.sum(p, axis=-1, keepdims=True)
    acc = alpha * acc + _dot(p.astype(BF16), vb)
    return m_new, l, acc


def _flash_init(tq):
    return (jnp.full((tq, 1), NEG_BIG, F32), jnp.zeros((tq, 1), F32), jnp.zeros((tq, LANES), F32))


def _head_pairs(q_ref, o_ref, attend):
    tq = q_ref.shape[0]
    lane = lax.broadcasted_iota(jnp.int32, (tq, LANES), 1)
    for pair in range(FOX_HEADS // 2):
        sl = slice(pair * LANES, (pair + 1) * LANES)
        q2 = q_ref[:, sl]
        outs = []
        for e in range(2):
            in_head = (lane < FOX_HEAD_DIM) if e == 0 else (lane >= FOX_HEAD_DIM)
            qm = jnp.where(in_head, q2, jnp.zeros_like(q2))
            _, l, acc = attend(2 * pair + e, sl, qm)
            outs.append(acc / l)
        o_ref[:, sl] = jnp.where(lane < FOX_HEAD_DIM, outs[0], outs[1]).astype(BF16)


def _fox_prompt_kernel(q_ref, k_ref, v_ref, cc_ref, cr_ref, o_ref,
                       kb_scr, vt_scr, cj_scr, qt_scr, m_scr, l_scr, acc_scr, *, tq):
    qi = pl.program_id(1)
    n_kb = vt_scr.shape[0]
    n_pair = FOX_HEADS // 2

    @pl.when(qi == 0)
    def _():
        for jb in range(n_kb):
            rows = slice(jb * tq, (jb + 1) * tq)
            kb_scr[rows, :] = k_ref[rows, :].astype(BF16)
            vt_scr[jb] = v_ref[rows, :].T.astype(BF16)
            for hd in range(FOX_HEADS):
                cj_scr[hd, rows, :] = jnp.broadcast_to(cc_ref[rows, hd:hd + 1], (tq, LANES))

    dim = lax.broadcasted_iota(jnp.int32, (LANES, tq), 0)
    for pair in range(n_pair):
        qt = q_ref[:, pair * LANES:(pair + 1) * LANES].astype(F32).T
        qt_scr[2 * pair] = jnp.where(dim < FOX_HEAD_DIM, qt, 0.0).astype(BF16)
        qt_scr[2 * pair + 1] = jnp.where(dim >= FOX_HEAD_DIM, qt, 0.0).astype(BF16)
    m_scr[...] = jnp.full(m_scr.shape, NEG_BIG, F32)
    l_scr[...] = jnp.zeros(l_scr.shape, F32)
    acc_scr[...] = jnp.zeros(acc_scr.shape, F32)
    crq = cr_ref[qi]

    def step(j, causal):
        rows = pl.ds(pl.multiple_of(j * tq, tq), tq)

        def logits(hd):
            kb = kb_scr[rows, (hd // 2) * LANES:(hd // 2 + 1) * LANES]
            return _dot(kb, qt_scr[hd])

        pending = [logits(hd) for hd in range(FOX_LOOKAHEAD)]
        for hd in range(FOX_HEADS):
            t = pending.pop(0) - jnp.concatenate([cj_scr[hd, rows, :]] * (tq // LANES), axis=1)
            if hd + FOX_LOOKAHEAD < FOX_HEADS:
                pending.append(logits(hd + FOX_LOOKAHEAD))
            if causal:
                key = lax.broadcasted_iota(jnp.int32, t.shape, 0)
                qry = lax.broadcasted_iota(jnp.int32, t.shape, 1)
                t = jnp.where(key <= qry, t, NEG_BIG)
            ci = crq[hd:hd + 1, :]
            m_old = m_scr[hd]
            m_new = jnp.maximum(m_old, ci + jnp.max(t, axis=0, keepdims=True))
            p = jnp.exp(t + (ci - m_new))
            alpha = jnp.exp(m_old - m_new)
            l_scr[hd] = alpha * l_scr[hd] + jnp.sum(p, axis=0, keepdims=True)
            vt = vt_scr[j, (hd // 2) * LANES:(hd // 2 + 1) * LANES, :]
            acc_scr[hd] = alpha * acc_scr[hd] + _dot(vt, p.astype(BF16))
            m_scr[hd] = m_new

    def body(j, carry):
        step(j, False)
        return carry

    lax.fori_loop(0, qi, body, 0)
    step(qi, True)

    for pair in range(n_pair):
        outs = [acc_scr[hd] / l_scr[hd] for hd in (2 * pair, 2 * pair + 1)]
        o_ref[:, pair * LANES:(pair + 1) * LANES] = jnp.where(dim < FOX_HEAD_DIM, outs[0], outs[1]).T.astype(BF16)


def _fox_prompt(fq, fk, fv, ccol, crow, *, n_seq, tq):
    n = fq.shape[0]
    s_len = n // n_seq
    n_q = s_len // tq
    return pl.pallas_call(
        functools.partial(_fox_prompt_kernel, tq=tq),
        grid=(n_seq, n_q),
        in_specs=[pl.BlockSpec((tq, FOX_W), lambda b, i: (b * n_q + i, 0)),
                  pl.BlockSpec((s_len, FOX_W), lambda b, i: (b, 0)),
                  pl.BlockSpec((s_len, FOX_W), lambda b, i: (b, 0)),
                  pl.BlockSpec((s_len, FOX_HEADS), lambda b, i: (b, 0)),
                  pl.BlockSpec((n_q, FOX_HEADS, tq), lambda b, i: (b, 0, 0))],
        out_specs=pl.BlockSpec((tq, FOX_W), lambda b, i: (b * n_q + i, 0)),
        out_shape=jax.ShapeDtypeStruct((n, FOX_W), BF16),
        scratch_shapes=[pltpu.VMEM((s_len, FOX_W), BF16), pltpu.VMEM((n_q, FOX_W, tq), BF16),
                        pltpu.VMEM((FOX_HEADS, s_len, LANES), F32), pltpu.VMEM((FOX_HEADS, LANES, tq), BF16),
                        pltpu.VMEM((FOX_HEADS, 1, tq), F32), pltpu.VMEM((FOX_HEADS, 1, tq), F32),
                        pltpu.VMEM((FOX_HEADS, LANES, tq), F32)],
        compiler_params=_cparams(2),
        name="fox_prompt",
    )(fq, fk, fv, ccol, crow)


def _fox_sample_kernel(q_ref, ck_ref, cv_ref, clf_ref, k_ref, v_ref, cc_ref, cr_ref, o_ref, cpast_scr, *, tk):
    tq = q_ref.shape[0]
    p_len = ck_ref.shape[0]

    x = clf_ref[0]
    lane = lax.broadcasted_iota(jnp.int32, x.shape, 1)
    sh = 1
    while sh < p_len:
        x = x + jnp.where(lane >= sh, pltpu.roll(x, sh, axis=1), 0.0)
        sh *= 2
    cpast_scr[...] = x - x[:, p_len - 1:p_len]

    def attend(hd, sl, qm):
        ci = cc_ref[:, hd:hd + 1]
        carry = _flash_init(tq)
        for j in range(p_len // tk):
            rows = slice(j * tk, (j + 1) * tk)
            carry = _flash_step(carry, qm, ck_ref[rows, sl].astype(BF16), cv_ref[rows, sl].astype(BF16),
                                ci, cpast_scr[hd:hd + 1, rows], False)
        return _flash_step(carry, qm, k_ref[:, sl].astype(BF16), v_ref[:, sl].astype(BF16),
                           ci, cr_ref[0][hd:hd + 1, :], True)

    _head_pairs(q_ref, o_ref, attend)


def _fox_sample(fq, ck, cv, clf_t, fk, fv, ccol, crow, *, n_seq, tk):
    n = fq.shape[0]
    l_new = n // n_seq
    p_len = ck.shape[0] // n_seq
    new = lambda w_: pl.BlockSpec((l_new, w_), lambda b: (b, 0))
    return pl.pallas_call(
        functools.partial(_fox_sample_kernel, tk=tk),
        grid=(n_seq,),
        in_specs=[new(FOX_W),
                  pl.BlockSpec((p_len, FOX_W), lambda b: (b, 0)),
                  pl.BlockSpec((p_len, FOX_W), lambda b: (b, 0)),
                  pl.BlockSpec((1, FOX_HEADS, p_len), lambda b: (b, 0, 0)),
                  new(FOX_W), new(FOX_W), new(FOX_HEADS),
                  pl.BlockSpec((1, FOX_HEADS, l_new), lambda b: (b, 0, 0))],
        out_specs=new(FOX_W),
        out_shape=jax.ShapeDtypeStruct((n, FOX_W), BF16),
        scratch_shapes=[pltpu.VMEM((FOX_HEADS, p_len), F32)],
        compiler_params=_cparams(1),
        name="fox_sample",
    )(fq, ck, cv, clf_t, fk, fv, ccol, crow)


def _post_kernel(x_ref, og_ref, of_ref, gt_ref, p_ref, wbr_ref, wbf_ref, wo_ref, g2_ref, win_ref, wout_ref,
                 gp_ref, wpg_ref, wple_ref, o_ref, a_scr, *, ff_chunk):
    d = x_ref.shape[1]
    br_ret = _dot(og_ref[...], wbr_ref[...])
    br_fox = _dot(of_ref[...], wbf_ref[...])
    merged = gt_ref[:, :d].astype(F32) * br_ret + gt_ref[:, d:].astype(F32) * br_fox
    x = x_ref[...] + _dot(merged.astype(BF16), wo_ref[...])
    x = x + 0.5 * _swiglu(x, g2_ref, win_ref, wout_ref, a_scr, ff_chunk)
    gate = jax.nn.sigmoid(_dot(_rms(x, gp_ref[...]).astype(BF16), wpg_ref[...]))
    o_ref[...] = x + _dot(p_ref[...].astype(BF16), wple_ref[...]) * gate


def _post(x, og, of, gates, p, wbr, wbf, wo, g2, w_in, w_out, gp, wpg, wple, *, tm, ff_chunk):
    n, d = x.shape
    d_ff = w_out.shape[0]
    row = lambda w_: pl.BlockSpec((tm, w_), lambda i: (i, 0))
    weights = [wbr, wbf, wo, g2, w_in, w_out, gp, wpg, wple]
    return pl.pallas_call(
        functools.partial(_post_kernel, ff_chunk=ff_chunk),
        grid=(n // tm,),
        in_specs=[row(d), row(og.shape[1]), row(of.shape[1]), row(gates.shape[1]), row(p.shape[1])]
        + [_resident(w.shape) for w in weights],
        out_specs=row(d),
        out_shape=jax.ShapeDtypeStruct((n, d), F32),
        scratch_shapes=[pltpu.VMEM((tm, d_ff), BF16)],
        compiler_params=_cparams(1),
        name="post",
    )(x, og, of, gates, p, *weights)


def _row_tile(n, seg, want):
    t = want
    while t > 8 and (n % t or (seg % t and t % seg)):
        t //= 2
    assert n % t == 0 and (seg % t == 0 or (t % seg == 0 and seg & (seg - 1) == 0)), (n, seg, t)
    return t


def _layer(x, p, s0, fox_cache, prm, *, ret_chunk, pos0):
    b, s, d = x.shape
    n = b * s
    xf = x.reshape(n, d)
    tm = _row_tile(n, s, 512)
    ff_chunk = 256

    x1 = _ffn1(xf, prm["g1"], prm["w1_in"], prm["w1_out"], tm, ff_chunk)

    is_prompt = fox_cache is None
    tq = 256 if is_prompt else s
    assert s % tq == 0 and tm % tq == 0
    rq, rk, rv, rg, fq, fk, fv, logf, ccol, crow, gates = _mix_in(
        x1, prm["gm"], prm["w_mix"], prm["b_forget"], prm["qg"], prm["kg"], prm["hsum"], prm["freq"],
        tm=tm, seg=s, pos0=pos0, c_unit=tq)

    t_blk = math.gcd(s, 256)
    assert t_blk % ret_chunk == 0 or ret_chunk % t_blk == 0
    og, state = _retention(rq, rk, rv, rg, s0, n_seq=b, t_blk=t_blk, chunk=min(ret_chunk, t_blk))

    if is_prompt:
        of = _fox_prompt(fq, fk, fv, ccol, crow, n_seq=b, tq=tq)
    else:
        ck, cv, clf = fox_cache
        p_len = ck.shape[1]
        of = _fox_sample(fq, ck.reshape(b * p_len, FOX_W), cv.reshape(b * p_len, FOX_W),
                         jnp.swapaxes(clf, 1, 2), fk, fv, ccol, crow, n_seq=b, tk=math.gcd(p_len, 1024))

    y = _post(x1, og, of, gates, p.reshape(n, -1), prm["w_br_ret"], prm["w_br_fox"], prm["w_out"], prm["g2"],
              prm["w2_in"], prm["w2_out"], prm["gp"], prm["w_ple_gate"], prm["w_ple"], tm=tm, ff_chunk=ff_chunk)
    return (y.reshape(b, s, d), state, fk.reshape(b, s, FOX_HEADS, FOX_HEAD_DIM),
            fv.reshape(b, s, FOX_HEADS, FOX_HEAD_DIM), logf.reshape(b, s, FOX_HEADS))


def _prep_params(i, norm_ffn1_g, ffn1_w_in, ffn1_w_out, norm_mix_g, w_in_mix, b_forget, q_norm_g, k_norm_g,
                 w_br_ret, w_br_fox, w_out, norm_ffn2_g, ffn2_w_in, ffn2_w_out, norm_ple_g, w_ple, w_ple_gate):
    wm = w_in_mix[i]
    o_ff = 2 * RET_QK_W + 2 * RET_V_W + 3 * FOX_W
    w_mix = jnp.concatenate(
        [wm[:, :o_ff], jnp.pad(wm[:, o_ff:o_ff + FOX_HEADS], ((0, 0), (0, LANES - FOX_HEADS))),
         wm[:, o_ff + FOX_HEADS:]], axis=1).astype(BF16)
    head_of = np.arange(FOX_W) // FOX_HEAD_DIM
    half = RET_KEY_DIM // 2
    freqs = ROPE_BASE ** (-jnp.arange(half, dtype=F32) / half)
    return {
        "g1": norm_ffn1_g[i][None, :], "w1_in": ffn1_w_in[i].astype(BF16), "w1_out": ffn1_w_out[i].astype(BF16),
        "gm": norm_mix_g[i][None, :], "w_mix": w_mix,
        "b_forget": jnp.pad(b_forget[i], (0, LANES - FOX_HEADS))[None, :].astype(F32),
        "qg": jnp.tile(q_norm_g[i], FOX_HEADS)[None, :], "kg": jnp.tile(k_norm_g[i], FOX_HEADS)[None, :],
        "hsum": jnp.asarray(head_of[:, None] == head_of[None, :], BF16),
        "freq": jnp.concatenate([freqs, freqs])[None, :],
        "w_br_ret": w_br_ret[i].astype(BF16), "w_br_fox": w_br_fox[i].astype(BF16), "w_out": w_out[i].astype(BF16),
        "g2": norm_ffn2_g[i][None, :], "w2_in": ffn2_w_in[i].astype(BF16), "w2_out": ffn2_w_out[i].astype(BF16),
        "gp": norm_ple_g[i][None, :], "w_ple_gate": w_ple_gate[i].astype(BF16), "w_ple": w_ple[i].astype(BF16),
    }


def kernel(x_prompt, x_sample, p_prompt, p_sample, state_ret, cache_fox_k, cache_fox_v, cache_fox_logf, norm_ffn1_g, ffn1_w_in, ffn1_w_out, norm_mix_g, w_in_mix, b_forget, q_norm_g, k_norm_g, w_br_ret, w_br_fox, w_out, norm_ffn2_g, ffn2_w_in, ffn2_w_out, norm_ple_g, w_ple, w_ple_gate):
    depth = p_prompt.shape[0]
    past_len = cache_fox_k.shape[2]
    dec_seq = x_sample.shape[1]
    xp, xs = x_prompt, x_sample
    outs_p, outs_s = [], []
    for i in range(depth):
        prm = _prep_params(i, norm_ffn1_g, ffn1_w_in, ffn1_w_out, norm_mix_g, w_in_mix, b_forget, q_norm_g,
                           k_norm_g, w_br_ret, w_br_fox, w_out, norm_ffn2_g, ffn2_w_in, ffn2_w_out, norm_ple_g,
                           w_ple, w_ple_gate)
        xp, *rest_p = _layer(xp, p_prompt[i], None, None, prm, ret_chunk=RET_CHUNK, pos0=0)
        xs, *rest_s = _layer(xs, p_sample[i], state_ret[i],
                             (cache_fox_k[i], cache_fox_v[i], cache_fox_logf[i]), prm,
                             ret_chunk=dec_seq, pos0=past_len)
        outs_p.append(rest_p)
        outs_s.append(rest_s)
    stack = lambda outs, j: jnp.stack([o[j] for o in outs])
    return (xp, xs, stack(outs_p, 0), stack(outs_p, 1), stack(outs_p, 2), stack(outs_p, 3),
            stack(outs_s, 0), stack(outs_s, 1), stack(outs_s, 2), stack(outs_s, 3))
```

```python
import functools
import math

import jax
import jax.numpy as jnp
import numpy as np
from jax import lax
from jax.experimental import pallas as pl
from jax.experimental.pallas import tpu as pltpu

F32 = jnp.float32
BF16 = jnp.bfloat16

EPS = 1e-6
ROPE_BASE = 10000.0
RET_CHUNK = 64
RET_HEADS = 4
RET_KEY_DIM = 128
RET_VAL_DIM = 256
FOX_HEADS = 8
FOX_HEAD_DIM = 64
N_BRANCH = 2
FOX_LOOKAHEAD = 5
LANES = 128
NEG_BIG = -1e30
VMEM_LIMIT_BYTES = 58 * 1024 * 1024

RET_QK_W = RET_HEADS * RET_KEY_DIM
RET_V_W = RET_HEADS * RET_VAL_DIM
FOX_W = FOX_HEADS * FOX_HEAD_DIM

NT_DIMS = (((1,), (1,)), ((), ()))
TN_DIMS = (((0,), (0,)), ((), ()))


def _cparams(n_axes):
    return pltpu.CompilerParams(dimension_semantics=("arbitrary",) * n_axes,
                                vmem_limit_bytes=VMEM_LIMIT_BYTES)


def _resident(shape):
    nd = len(shape)
    return pl.BlockSpec(shape, lambda *_: (0,) * nd, pipeline_mode=pl.Buffered(1))


def _rms(x, g=None):
    y = x * lax.rsqrt(jnp.mean(x * x, axis=-1, keepdims=True) + EPS)
    return y if g is None else y * g


def _dot(a, b):
    return jnp.dot(a, b, preferred_element_type=F32)


def _split3(x):
    hi = x.astype(BF16)
    r1 = x - hi.astype(F32)
    mid = r1.astype(BF16)
    lo = (r1 - mid.astype(F32)).astype(BF16)
    return hi, mid, lo


def _swiglu(x, g_ref, win_ref, wout_ref, a_scr, ff_chunk):
    d_ff = wout_ref.shape[0]
    h = _rms(x, g_ref[...]).astype(BF16)
    for c in range(d_ff // ff_chunk):
        lo = c * ff_chunk
        g = _dot(h, win_ref[:, lo:lo + ff_chunk])
        u = _dot(h, win_ref[:, d_ff + lo:d_ff + lo + ff_chunk])
        a_scr[:, lo:lo + ff_chunk] = (g * jax.nn.sigmoid(g) * u).astype(BF16)
    return _dot(a_scr[...], wout_ref[...])


def _ffn1_kernel(x_ref, g_ref, win_ref, wout_ref, o_ref, a_scr, *, ff_chunk):
    x = x_ref[...]
    o_ref[...] = x + 0.5 * _swiglu(x, g_ref, win_ref, wout_ref, a_scr, ff_chunk)


def _ffn1(x, g, w_in, w_out, tm, ff_chunk):
    n, d = x.shape
    d_ff = w_out.shape[0]
    return pl.pallas_call(
        functools.partial(_ffn1_kernel, ff_chunk=ff_chunk),
        grid=(n // tm,),
        in_specs=[pl.BlockSpec((tm, d), lambda i: (i, 0)),
                  _resident(g.shape), _resident(w_in.shape), _resident(w_out.shape)],
        out_specs=pl.BlockSpec((tm, d), lambda i: (i, 0)),
        out_shape=jax.ShapeDtypeStruct((n, d), F32),
        scratch_shapes=[pltpu.VMEM((tm, d_ff), BF16)],
        compiler_params=_cparams(1),
        name="ffn1",
    )(x, g, w_in, w_out)


_O_RQ = 0
_O_RK = _O_RQ + RET_QK_W
_O_RV = _O_RK + RET_QK_W
_O_RG = _O_RV + RET_V_W
_O_FQ = _O_RG + RET_V_W
_O_FK = _O_FQ + FOX_W
_O_FV = _O_FK + FOX_W
_O_FF = _O_FV + FOX_W
_O_ZG = _O_FF + LANES


def _mix_in_kernel(x_ref, g_ref, w_ref, bf_ref, qg_ref, kg_ref, hsum_ref, freq_ref,
                   rq_ref, rk_ref, rv_ref, rg_ref, fq_ref, kb_ref, vb_ref, fk_ref, fv_ref, lf_ref, cc_ref, cr_ref, gt_ref,
                   cos_scr, sin_scr, tri_scr, carry_scr, *, tm, seg, pos0, c_unit):
    i = pl.program_id(0)
    n_pos_tiles = max(seg // tm, 1)
    d_model = x_ref.shape[1]

    @pl.when(i == 0)
    def _():
        r = lax.broadcasted_iota(jnp.int32, (tm, tm), 0)
        c = lax.broadcasted_iota(jnp.int32, (tm, tm), 1)
        keep = c <= r
        if seg < tm:
            keep = keep & ((r & -seg) == (c & -seg))
        tri_scr[...] = jnp.where(keep, 1.0, 0.0).astype(BF16)

    slot = i % n_pos_tiles

    @pl.when(i < n_pos_tiles)
    def _():
        row = lax.broadcasted_iota(jnp.int32, (tm, LANES), 0)
        row = row + slot * tm if seg >= tm else row & (seg - 1)
        pos = (pos0 + row).astype(F32)
        ang = pos * freq_ref[...]
        lane = lax.broadcasted_iota(jnp.int32, (tm, LANES), 1)
        cos_scr[pl.ds(pl.multiple_of(slot * tm, tm), tm), :] = jnp.cos(ang)
        sin_scr[pl.ds(pl.multiple_of(slot * tm, tm), tm), :] = jnp.where(lane < LANES // 2, -1.0, 1.0) * jnp.sin(ang)

    cos = cos_scr[pl.ds(pl.multiple_of(slot * tm, tm), tm), :]
    sin = sin_scr[pl.ds(pl.multiple_of(slot * tm, tm), tm), :]

    h = _rms(x_ref[...], g_ref[...]).astype(BF16)

    def rotary(z, scale):
        out = z * cos + pltpu.roll(z, LANES // 2, axis=1) * sin
        return out if scale is None else out * scale

    def proj(lo, width):
        return _dot(h, w_ref[:, lo:lo + width])

    def head_sumsq(z):
        return _dot((z * z).astype(BF16), hsum_ref[...])

    def head_norm(z, ss, gain):
        return z * lax.rsqrt(ss * (1.0 / FOX_HEAD_DIM) + EPS) * gain

    def store_heads(ref, z):
        for hd in range(FOX_HEADS):
            ref[pl.ds(hd, tm, stride=FOX_HEADS), :] = z[:, hd * FOX_HEAD_DIM:(hd + 1) * FOX_HEAD_DIM]

    zf = proj(_O_FF, LANES) + bf_ref[...]
    zfq = proj(_O_FQ, FOX_W)
    zfk = proj(_O_FK, FOX_W)

    zq = proj(_O_RQ, RET_QK_W)
    for hd in range(RET_HEADS):
        sl = slice(hd * RET_KEY_DIM, (hd + 1) * RET_KEY_DIM)
        rq_ref[:, sl] = rotary(zq[:, sl], None).astype(BF16)

    lane = lax.broadcasted_iota(jnp.int32, (tm, LANES), 1)
    lf = jnp.where(lane < FOX_HEADS, jnp.minimum(zf, 0.0) - jnp.log1p(jnp.exp(-jnp.abs(zf))), 0.0)
    lf_ref[...] = lf[:, :FOX_HEADS]
    ss_q = head_sumsq(zfq)
    csum = sum(_dot(tri_scr[...], part) for part in _split3(lf))
    ss_k = head_sumsq(zfk)

    zk = proj(_O_RK, RET_QK_W)
    for hd in range(RET_HEADS):
        sl = slice(hd * RET_KEY_DIM, (hd + 1) * RET_KEY_DIM)
        rk_ref[:, sl] = rotary(zk[:, sl], RET_KEY_DIM ** -0.5).astype(BF16)

    fq_ref[...] = (head_norm(zfq, ss_q, qg_ref[...]) * (FOX_HEAD_DIM ** -0.5)).astype(BF16)
    fk = head_norm(zfk, ss_k, kg_ref[...])
    kb_ref[...] = fk.astype(BF16)
    store_heads(fk_ref, fk)

    if seg >= tm:
        seq_start = (i * tm) % seg == 0
        csum = csum + jnp.where(seq_start, 0.0, carry_scr[...])
        carry_scr[...] = csum[tm - 1:tm, :]
    cc_ref[...] = csum[:, :FOX_HEADS]

    rv_ref[...] = proj(_O_RV, RET_V_W).astype(BF16)

    er = lax.broadcasted_iota(jnp.int32, (LANES, LANES), 0)
    ec = lax.broadcasted_iota(jnp.int32, (LANES, LANES), 1)
    eye = jnp.where(er == ec, 1.0, 0.0).astype(BF16)
    csum_t = sum(lax.dot_general(eye, part, NT_DIMS, preferred_element_type=F32) for part in _split3(csum))
    for u in range(tm // c_unit):
        cr_ref[u] = csum_t[:FOX_HEADS, u * c_unit:(u + 1) * c_unit]

    zg_ret = proj(_O_RG, RET_V_W)
    rg_ref[...] = (zg_ret * jax.nn.sigmoid(zg_ret)).astype(BF16)
    fv = proj(_O_FV, FOX_W)
    vb_ref[...] = fv.astype(BF16)
    store_heads(fv_ref, fv)
    for c in range(N_BRANCH):
        gt_ref[:, c * d_model:(c + 1) * d_model] = jax.nn.sigmoid(proj(_O_ZG + c * d_model, d_model)).astype(BF16)


def _mix_in(x, g, w, b_pad, qg, kg, hsum, freq, *, tm, seg, pos0, c_unit):
    n, d = x.shape
    n_pos_tiles = max(seg // tm, 1)
    row = lambda w_: pl.BlockSpec((tm, w_), lambda i: (i, 0))
    out_shapes = [
        jax.ShapeDtypeStruct((n, RET_QK_W), BF16), jax.ShapeDtypeStruct((n, RET_QK_W), BF16),
        jax.ShapeDtypeStruct((n, RET_V_W), BF16), jax.ShapeDtypeStruct((n, RET_V_W), BF16),
        jax.ShapeDtypeStruct((n, FOX_W), BF16), jax.ShapeDtypeStruct((n, FOX_W), BF16),
        jax.ShapeDtypeStruct((n, FOX_W), BF16),
        jax.ShapeDtypeStruct((n * FOX_HEADS, FOX_HEAD_DIM), F32),
        jax.ShapeDtypeStruct((n * FOX_HEADS, FOX_HEAD_DIM), F32), jax.ShapeDtypeStruct((n, FOX_HEADS), F32),
        jax.ShapeDtypeStruct((n, FOX_HEADS), F32), jax.ShapeDtypeStruct((n // c_unit, FOX_HEADS, c_unit), F32),
        jax.ShapeDtypeStruct((n, N_BRANCH * d), BF16),
    ]
    heads_spec = pl.BlockSpec((tm * FOX_HEADS, FOX_HEAD_DIM), lambda i: (i, 0))
    out_specs = [row(RET_QK_W), row(RET_QK_W), row(RET_V_W), row(RET_V_W), row(FOX_W), row(FOX_W), row(FOX_W),
                 heads_spec, heads_spec, row(FOX_HEADS), row(FOX_HEADS),
                 pl.BlockSpec((tm // c_unit, FOX_HEADS, c_unit), lambda i: (i, 0, 0)),
                 row(N_BRANCH * d)]
    return pl.pallas_call(
        functools.partial(_mix_in_kernel, tm=tm, seg=seg, pos0=pos0, c_unit=c_unit),
        grid=(n // tm,),
        in_specs=[row(d), _resident(g.shape), _resident(w.shape), _resident(b_pad.shape), _resident(qg.shape),
                  _resident(kg.shape), _resident(hsum.shape), _resident(freq.shape)],
        out_specs=out_specs,
        out_shape=out_shapes,
        scratch_shapes=[pltpu.VMEM((n_pos_tiles * tm, LANES), F32), pltpu.VMEM((n_pos_tiles * tm, LANES), F32),
                        pltpu.VMEM((tm, tm), BF16), pltpu.VMEM((1, LANES), F32)],
        compiler_params=_cparams(1),
        name="mix_in",
    )(x, g, w, b_pad, qg, kg, hsum, freq)


def _retention_kernel(*refs, t_blk, chunk, has_s0):
    if has_s0:
        q_ref, k_ref, v_ref, rg_ref, s0_ref, og_ref, st_ref, s_scr, m_scr = refs
    else:
        q_ref, k_ref, v_ref, rg_ref, og_ref, st_ref, s_scr, m_scr = refs
        s0_ref = None
    b, t = pl.program_id(0), pl.program_id(1)
    n_t = pl.num_programs(1)
    log_gamma = [math.log(1.0 - 2.0 ** (-5.0 - hd)) for hd in range(RET_HEADS)]
    shift = chunk.bit_length() - 1

    @pl.when((b == 0) & (t == 0))
    def _():
        r = lax.broadcasted_iota(jnp.int32, (t_blk, t_blk), 0)
        c = lax.broadcasted_iota(jnp.int32, (t_blk, t_blk), 1)
        keep = ((r >> shift) == (c >> shift)) | (c < r)
        dist = jnp.abs(r - c).astype(F32)
        for hd in range(RET_HEADS):
            m_scr[hd] = jnp.where(keep, jnp.exp(log_gamma[hd] * dist), 0.0)

    @pl.when(t == 0)
    def _():
        if has_s0:
            s_scr[...] = s0_ref[0]
        else:
            s_scr[...] = jnp.zeros_like(s_scr)

    idx = lax.broadcasted_iota(jnp.int32, (t_blk, 1), 0).astype(F32)
    heads = range(RET_HEADS)
    qs = [q_ref[:, hd * RET_KEY_DIM:(hd + 1) * RET_KEY_DIM] for hd in heads]
    ks = [k_ref[:, hd * RET_KEY_DIM:(hd + 1) * RET_KEY_DIM] for hd in heads]
    vs = [v_ref[:, hd * RET_VAL_DIM:(hd + 1) * RET_VAL_DIM] for hd in heads]
    states = [s_scr[hd] for hd in heads]
    atts = [lax.dot_general(qs[hd], ks[hd], NT_DIMS, preferred_element_type=F32) for hd in heads]
    inters = [_dot(qs[hd], states[hd].astype(BF16)) for hd in heads]
    kds = [(ks[hd].astype(F32) * jnp.exp(log_gamma[hd] * (t_blk - 1.0 - idx))).astype(BF16) for hd in heads]
    updates = [lax.dot_general(kds[hd], vs[hd], TN_DIMS, preferred_element_type=F32) for hd in heads]
    for hd in heads:
        lg = log_gamma[hd]
        o = _dot((atts[hd] * m_scr[hd]).astype(BF16), vs[hd]) + inters[hd] * jnp.exp(lg * (idx + 1.0))
        new_state = states[hd] * math.exp(lg * t_blk) + updates[hd]
        s_scr[hd] = new_state
        gate = rg_ref[:, hd * RET_VAL_DIM:(hd + 1) * RET_VAL_DIM].astype(F32)
        og_ref[:, hd * RET_VAL_DIM:(hd + 1) * RET_VAL_DIM] = (_rms(o) * gate).astype(BF16)

        @pl.when(t == n_t - 1)
        def _():
            st_ref[0, hd] = new_state


def _retention(q, k, v, rg, s0, *, n_seq, t_blk, chunk):
    n = q.shape[0]
    n_t = n // n_seq // t_blk
    row = lambda w_: pl.BlockSpec((t_blk, w_), lambda b, t: (b * n_t + t, 0))
    st_shape = (n_seq, RET_HEADS, RET_KEY_DIM, RET_VAL_DIM)
    st_spec = pl.BlockSpec((1,) + st_shape[1:], lambda b, t: (b, 0, 0, 0))
    in_specs = [row(RET_QK_W), row(RET_QK_W), row(RET_V_W), row(RET_V_W)]
    args = [q, k, v, rg]
    if s0 is not None:
        in_specs.append(st_spec)
        args.append(s0)
    return pl.pallas_call(
        functools.partial(_retention_kernel, t_blk=t_blk, chunk=chunk, has_s0=s0 is not None),
        grid=(n_seq, n_t),
        in_specs=in_specs,
        out_specs=[row(RET_V_W), st_spec],
        out_shape=[jax.ShapeDtypeStruct((n, RET_V_W), BF16), jax.ShapeDtypeStruct(st_shape, F32)],
        scratch_shapes=[pltpu.VMEM(st_shape[1:], F32), pltpu.VMEM((RET_HEADS, t_blk, t_blk), F32)],
        compiler_params=_cparams(2),
        name="retention",
    )(*args)


def _flash_logits(qm, kb):
    return lax.dot_general(qm, kb, NT_DIMS, preferred_element_type=F32)


def _flash_update(carry, s, vb, ci, cj, causal):
    m, l, acc = carry
    s = s + (ci - cj)
    if causal:
        r = lax.broadcasted_iota(jnp.int32, s.shape, 0)
        c = lax.broadcasted_iota(jnp.int32, s.shape, 1)
        s = jnp.where(c <= r, s, NEG_BIG)
    m_new = jnp.maximum(m, jnp.max(s, axis=-1, keepdims=True))
    alpha = jnp.exp(m - m_new)
    p = jnp.exp(s - m_new)
    l = alpha * l + jnp.sum(p, axis=-1, keepdims=True)
    acc = alpha * acc + _dot(p.astype(BF16), vb)
    return m_new, l, acc


def _fox_prompt_kernel(q_ref, k_ref, v_ref, cc_ref, cr_ref, o_ref,
                       vt_scr, cj_scr, qt_scr, m_scr, l_scr, acc_scr, *, tq):
    qi = pl.program_id(1)
    n_kb = vt_scr.shape[0]
    n_pair = FOX_HEADS // 2

    @pl.when(qi == 0)
    def _():
        for jb in range(n_kb):
            rows = slice(jb * tq, (jb + 1) * tq)
            vt_scr[jb] = v_ref[rows, :].astype(F32).T.astype(BF16)
            for hd in range(FOX_HEADS):
                cj_scr[hd, rows, :] = jnp.broadcast_to(cc_ref[rows, hd:hd + 1], (tq, LANES))

    dim = lax.broadcasted_iota(jnp.int32, (LANES, tq), 0)
    for pair in range(n_pair):
        qt = q_ref[:, pair * LANES:(pair + 1) * LANES].astype(F32).T
        qt_scr[2 * pair] = jnp.where(dim < FOX_HEAD_DIM, qt, 0.0).astype(BF16)
        qt_scr[2 * pair + 1] = jnp.where(dim >= FOX_HEAD_DIM, qt, 0.0).astype(BF16)
    m_scr[...] = jnp.full(m_scr.shape, NEG_BIG, F32)
    l_scr[...] = jnp.zeros(l_scr.shape, F32)
    acc_scr[...] = jnp.zeros(acc_scr.shape, F32)
    crq = cr_ref[qi]

    def step(j, causal):
        rows = pl.ds(pl.multiple_of(j * tq, tq), tq)

        def logits(hd):
            kb = k_ref[rows, (hd // 2) * LANES:(hd // 2 + 1) * LANES]
            return _dot(kb, qt_scr[hd])

        pending = [logits(hd) for hd in range(FOX_LOOKAHEAD)]
        for hd in range(FOX_HEADS):
            t = pending.pop(0) - jnp.concatenate([cj_scr[hd, rows, :]] * (tq // LANES), axis=1)
            if hd + FOX_LOOKAHEAD < FOX_HEADS:
                pending.append(logits(hd + FOX_LOOKAHEAD))
            if causal:
                key = lax.broadcasted_iota(jnp.int32, t.shape, 0)
                qry = lax.broadcasted_iota(jnp.int32, t.shape, 1)
                t = jnp.where(key <= qry, t, NEG_BIG)
            ci = crq[hd:hd + 1, :]
            m_old = m_scr[hd]
            m_new = jnp.maximum(m_old, ci + jnp.max(t, axis=0, keepdims=True))
            p = jnp.exp(t + (ci - m_new))
            alpha = jnp.exp(m_old - m_new)
            l_scr[hd] = alpha * l_scr[hd] + jnp.sum(p, axis=0, keepdims=True)
            vt = vt_scr[j, (hd // 2) * LANES:(hd // 2 + 1) * LANES, :]
            acc_scr[hd] = alpha * acc_scr[hd] + _dot(vt, p.astype(BF16))
            m_scr[hd] = m_new

    def body(j, carry):
        step(j, False)
        return carry

    lax.fori_loop(0, qi, body, 0)
    step(qi, True)

    for pair in range(n_pair):
        outs = [acc_scr[hd] / l_scr[hd] for hd in (2 * pair, 2 * pair + 1)]
        o_ref[:, pair * LANES:(pair + 1) * LANES] = jnp.where(dim < FOX_HEAD_DIM, outs[0], outs[1]).T.astype(BF16)


def _fox_prompt(fq, fk, fv, ccol, crow, *, n_seq, tq):
    n = fq.shape[0]
    s_len = n // n_seq
    n_q = s_len // tq
    return pl.pallas_call(
        functools.partial(_fox_prompt_kernel, tq=tq),
        grid=(n_seq, n_q),
        in_specs=[pl.BlockSpec((tq, FOX_W), lambda b, i: (b * n_q + i, 0)),
                  pl.BlockSpec((s_len, FOX_W), lambda b, i: (b, 0)),
                  pl.BlockSpec((s_len, FOX_W), lambda b, i: (b, 0)),
                  pl.BlockSpec((s_len, FOX_HEADS), lambda b, i: (b, 0)),
                  pl.BlockSpec((n_q, FOX_HEADS, tq), lambda b, i: (b, 0, 0))],
        out_specs=pl.BlockSpec((tq, FOX_W), lambda b, i: (b * n_q + i, 0)),
        out_shape=jax.ShapeDtypeStruct((n, FOX_W), BF16),
        scratch_shapes=[pltpu.VMEM((n_q, FOX_W, tq), BF16),
                        pltpu.VMEM((FOX_HEADS, s_len, LANES), F32), pltpu.VMEM((FOX_HEADS, LANES, tq), BF16),
                        pltpu.VMEM((FOX_HEADS, 1, tq), F32), pltpu.VMEM((FOX_HEADS, 1, tq), F32),
                        pltpu.VMEM((FOX_HEADS, LANES, tq), F32)],
        compiler_params=_cparams(2),
        name="fox_prompt",
    )(fq, fk, fv, ccol, crow)


def _fox_sample_kernel(q_ref, ck_ref, cv_ref, clf_ref, k_ref, v_ref, cc_ref, cr_ref, o_ref,
                       cpast_scr, kf_scr, vf_scr, m_scr, l_scr, acc_scr, *, tk):
    tq = q_ref.shape[0]
    j = pl.program_id(1)
    n_cb = pl.num_programs(1)
    p_len = clf_ref.shape[2]

    @pl.when(j == 0)
    def _():
        x = clf_ref[0]
        lane = lax.broadcasted_iota(jnp.int32, x.shape, 1)
        sh = 1
        while sh < p_len:
            x = x + jnp.where(lane >= sh, pltpu.roll(x, sh, axis=1), 0.0)
            sh *= 2
        x = x - x[:, p_len - 1:p_len]
        for jb in range(p_len // tk):
            cpast_scr[jb] = x[:, jb * tk:(jb + 1) * tk]
        m_scr[...] = jnp.full(m_scr.shape, NEG_BIG, F32)
        l_scr[...] = jnp.zeros(l_scr.shape, F32)
        acc_scr[...] = jnp.zeros(acc_scr.shape, F32)

    for hd in range(FOX_HEADS):
        kf_scr[:, hd * FOX_HEAD_DIM:(hd + 1) * FOX_HEAD_DIM] = ck_ref[pl.ds(hd, tk, stride=FOX_HEADS), :]
        vf_scr[:, hd * FOX_HEAD_DIM:(hd + 1) * FOX_HEAD_DIM] = cv_ref[pl.ds(hd, tk, stride=FOX_HEADS), :]

    lane = lax.broadcasted_iota(jnp.int32, (tq, LANES), 1)
    cpast = cpast_scr[j]

    def masked_q(hd):
        q2 = q_ref[:, (hd // 2) * LANES:(hd // 2 + 1) * LANES]
        in_head = (lane < FOX_HEAD_DIM) if hd % 2 == 0 else (lane >= FOX_HEAD_DIM)
        return jnp.where(in_head, q2, jnp.zeros_like(q2))

    pair_sl = [slice(pair * LANES, (pair + 1) * LANES) for pair in range(FOX_HEADS // 2)]
    qms = [masked_q(hd) for hd in range(FOX_HEADS)]
    logits = [_flash_logits(qms[hd], kf_scr[:, pair_sl[hd // 2]].astype(BF16)) for hd in range(FOX_HEADS)]
    for hd in range(FOX_HEADS):
        carry = _flash_update((m_scr[hd], l_scr[hd], acc_scr[hd]), logits[hd],
                              vf_scr[:, pair_sl[hd // 2]].astype(BF16), cc_ref[:, hd:hd + 1], cpast[hd:hd + 1, :],
                              False)
        m_scr[hd], l_scr[hd], acc_scr[hd] = carry

    @pl.when(j == n_cb - 1)
    def _():
        outs = []
        new_logits = [_flash_logits(qms[hd], k_ref[:, pair_sl[hd // 2]]) for hd in range(FOX_HEADS)]
        for hd in range(FOX_HEADS):
            _, l, acc = _flash_update((m_scr[hd], l_scr[hd], acc_scr[hd]), new_logits[hd],
                                      v_ref[:, pair_sl[hd // 2]], cc_ref[:, hd:hd + 1], cr_ref[0][hd:hd + 1, :], True)
            outs.append(acc / l)
        for pair in range(FOX_HEADS // 2):
            o_ref[:, pair * LANES:(pair + 1) * LANES] = jnp.where(
                lane < FOX_HEAD_DIM, outs[2 * pair], outs[2 * pair + 1]).astype(BF16)


def _fox_sample(fq, ck, cv, clf_t, kb, vb, ccol, crow, *, n_seq, tk):
    n = fq.shape[0]
    l_new = n // n_seq
    p_len = clf_t.shape[2]
    n_cb = p_len // tk
    new = lambda w_: pl.BlockSpec((l_new, w_), lambda b, j: (b, 0))
    cache = pl.BlockSpec((tk * FOX_HEADS, FOX_HEAD_DIM), lambda b, j: (b * n_cb + j, 0))
    return pl.pallas_call(
        functools.partial(_fox_sample_kernel, tk=tk),
        grid=(n_seq, n_cb),
        in_specs=[new(FOX_W), cache, cache,
                  pl.BlockSpec((1, FOX_HEADS, p_len), lambda b, j: (b, 0, 0)),
                  new(FOX_W), new(FOX_W), new(FOX_HEADS),
                  pl.BlockSpec((1, FOX_HEADS, l_new), lambda b, j: (b, 0, 0))],
        out_specs=new(FOX_W),
        out_shape=jax.ShapeDtypeStruct((n, FOX_W), BF16),
        scratch_shapes=[pltpu.VMEM((n_cb, FOX_HEADS, tk), F32), pltpu.VMEM((tk, FOX_W), F32),
                        pltpu.VMEM((tk, FOX_W), F32), pltpu.VMEM((FOX_HEADS, l_new, 1), F32),
                        pltpu.VMEM((FOX_HEADS, l_new, 1), F32), pltpu.VMEM((FOX_HEADS, l_new, LANES), F32)],
        compiler_params=_cparams(2),
        name="fox_sample",
    )(fq, ck, cv, clf_t, kb, vb, ccol, crow)


def _post_kernel(x_ref, og_ref, of_ref, gt_ref, p_ref, wbr_ref, wbf_ref, wo_ref, g2_ref, win_ref, wout_ref,
                 gp_ref, wpg_ref, wple_ref, o_ref, a_scr, *, ff_chunk):
    d = x_ref.shape[1]
    br_ret = _dot(og_ref[...], wbr_ref[...])
    br_fox = _dot(of_ref[...], wbf_ref[...])
    merged = gt_ref[:, :d].astype(F32) * br_ret + gt_ref[:, d:].astype(F32) * br_fox
    x = x_ref[...] + _dot(merged.astype(BF16), wo_ref[...])
    x = x + 0.5 * _swiglu(x, g2_ref, win_ref, wout_ref, a_scr, ff_chunk)
    gate = jax.nn.sigmoid(_dot(_rms(x, gp_ref[...]).astype(BF16), wpg_ref[...]))
    o_ref[...] = x + _dot(p_ref[...].astype(BF16), wple_ref[...]) * gate


def _post(x, og, of, gates, p, wbr, wbf, wo, g2, w_in, w_out, gp, wpg, wple, *, tm, ff_chunk):
    n, d = x.shape
    d_ff = w_out.shape[0]
    row = lambda w_: pl.BlockSpec((tm, w_), lambda i: (i, 0))
    weights = [wbr, wbf, wo, g2, w_in, w_out, gp, wpg, wple]
    return pl.pallas_call(
        functools.partial(_post_kernel, ff_chunk=ff_chunk),
        grid=(n // tm,),
        in_specs=[row(d), row(og.shape[1]), row(of.shape[1]), row(gates.shape[1]), row(p.shape[1])]
        + [_resident(w.shape) for w in weights],
        out_specs=row(d),
        out_shape=jax.ShapeDtypeStruct((n, d), F32),
        scratch_shapes=[pltpu.VMEM((tm, d_ff), BF16)],
        compiler_params=_cparams(1),
        name="post",
    )(x, og, of, gates, p, *weights)


def _row_tile(n, seg, want):
    t = want
    while t > 8 and (n % t or (seg % t and t % seg)):
        t //= 2
    assert n % t == 0 and (seg % t == 0 or (t % seg == 0 and seg & (seg - 1) == 0)), (n, seg, t)
    return t


def _layer(x, p, s0, fox_cache, prm, *, ret_chunk, pos0):
    b, s, d = x.shape
    n = b * s
    xf = x.reshape(n, d)
    tm = _row_tile(n, s, 512)
    ff_chunk = 256

    x1 = _ffn1(xf, prm["g1"], prm["w1_in"], prm["w1_out"], tm, ff_chunk)

    is_prompt = fox_cache is None
    tq = 256 if is_prompt else s
    assert s % tq == 0 and tm % tq == 0
    rq, rk, rv, rg, fq, kb, vb, fk, fv, logf, ccol, crow, gates = _mix_in(
        x1, prm["gm"], prm["w_mix"], prm["b_forget"], prm["qg"], prm["kg"], prm["hsum"], prm["freq"],
        tm=tm, seg=s, pos0=pos0, c_unit=tq)

    t_blk = math.gcd(s, 256)
    assert t_blk % ret_chunk == 0 or ret_chunk % t_blk == 0
    og, state = _retention(rq, rk, rv, rg, s0, n_seq=b, t_blk=t_blk, chunk=min(ret_chunk, t_blk))

    if is_prompt:
        of = _fox_prompt(fq, kb, vb, ccol, crow, n_seq=b, tq=tq)
    else:
        ck, cv, clf = fox_cache
        p_len = ck.shape[1]
        of = _fox_sample(fq, ck.reshape(b * p_len * FOX_HEADS, FOX_HEAD_DIM),
                         cv.reshape(b * p_len * FOX_HEADS, FOX_HEAD_DIM),
                         jnp.swapaxes(clf, 1, 2), kb, vb, ccol, crow, n_seq=b, tk=math.gcd(p_len, 1024))

    y = _post(x1, og, of, gates, p.reshape(n, -1), prm["w_br_ret"], prm["w_br_fox"], prm["w_out"], prm["g2"],
              prm["w2_in"], prm["w2_out"], prm["gp"], prm["w_ple_gate"], prm["w_ple"], tm=tm, ff_chunk=ff_chunk)
    return (y.reshape(b, s, d), state, fk.reshape(b, s, FOX_HEADS, FOX_HEAD_DIM),
            fv.reshape(b, s, FOX_HEADS, FOX_HEAD_DIM), logf.reshape(b, s, FOX_HEADS))


def _prep_params(i, norm_ffn1_g, ffn1_w_in, ffn1_w_out, norm_mix_g, w_in_mix, b_forget, q_norm_g, k_norm_g,
                 w_br_ret, w_br_fox, w_out, norm_ffn2_g, ffn2_w_in, ffn2_w_out, norm_ple_g, w_ple, w_ple_gate):
    wm = w_in_mix[i]
    o_ff = 2 * RET_QK_W + 2 * RET_V_W + 3 * FOX_W
    w_mix = jnp.concatenate(
        [wm[:, :o_ff], jnp.pad(wm[:, o_ff:o_ff + FOX_HEADS], ((0, 0), (0, LANES - FOX_HEADS))),
         wm[:, o_ff + FOX_HEADS:]], axis=1).astype(BF16)
    head_of = np.arange(FOX_W) // FOX_HEAD_DIM
    half = RET_KEY_DIM // 2
    freqs = ROPE_BASE ** (-jnp.arange(half, dtype=F32) / half)
    return {
        "g1": norm_ffn1_g[i][None, :], "w1_in": ffn1_w_in[i].astype(BF16), "w1_out": ffn1_w_out[i].astype(BF16),
        "gm": norm_mix_g[i][None, :], "w_mix": w_mix,
        "b_forget": jnp.pad(b_forget[i], (0, LANES - FOX_HEADS))[None, :].astype(F32),
        "qg": jnp.tile(q_norm_g[i], FOX_HEADS)[None, :], "kg": jnp.tile(k_norm_g[i], FOX_HEADS)[None, :],
        "hsum": jnp.asarray(head_of[:, None] == head_of[None, :], BF16),
        "freq": jnp.concatenate([freqs, freqs])[None, :],
        "w_br_ret": w_br_ret[i].astype(BF16), "w_br_fox": w_br_fox[i].astype(BF16), "w_out": w_out[i].astype(BF16),
        "g2": norm_ffn2_g[i][None, :], "w2_in": ffn2_w_in[i].astype(BF16), "w2_out": ffn2_w_out[i].astype(BF16),
        "gp": norm_ple_g[i][None, :], "w_ple_gate": w_ple_gate[i].astype(BF16), "w_ple": w_ple[i].astype(BF16),
    }


def kernel(x_prompt, x_sample, p_prompt, p_sample, state_ret, cache_fox_k, cache_fox_v, cache_fox_logf, norm_ffn1_g, ffn1_w_in, ffn1_w_out, norm_mix_g, w_in_mix, b_forget, q_norm_g, k_norm_g, w_br_ret, w_br_fox, w_out, norm_ffn2_g, ffn2_w_in, ffn2_w_out, norm_ple_g, w_ple, w_ple_gate):
    depth = p_prompt.shape[0]
    past_len = cache_fox_k.shape[2]
    dec_seq = x_sample.shape[1]
    xp, xs = x_prompt, x_sample
    outs_p, outs_s = [], []
    for i in range(depth):
        prm = _prep_params(i, norm_ffn1_g, ffn1_w_in, ffn1_w_out, norm_mix_g, w_in_mix, b_forget, q_norm_g,
                           k_norm_g, w_br_ret, w_br_fox, w_out, norm_ffn2_g, ffn2_w_in, ffn2_w_out, norm_ple_g,
                           w_ple, w_ple_gate)
        xp, *rest_p = _layer(xp, p_prompt[i], None, None, prm, ret_chunk=RET_CHUNK, pos0=0)
        xs, *rest_s = _layer(xs, p_sample[i], state_ret[i],
                             (cache_fox_k[i], cache_fox_v[i], cache_fox_logf[i]), prm,
                             ret_chunk=dec_seq, pos0=past_len)
        outs_p.append(rest_p)
        outs_s.append(rest_s)
    stack = lambda outs, j: jnp.stack([o[j] for o in outs])
    return (xp, xs, stack(outs_p, 0), stack(outs_p, 1), stack(outs_p, 2), stack(outs_p, 3),
            stack(outs_s, 0), stack(outs_s, 1), stack(outs_s, 2), stack(outs_s, 3))
```
